```python
import jax, jax.numpy as jnp
from jax import lax
import numpy as np

D_MODEL = 1024
BATCH = 8
SEQ = 8192
DEPTH = 2

N_EVEN = (DEPTH + 1) // 2
N_ODD = DEPTH // 2
BLOCK = 128
D_FF = 2816
MIX_WIDTH = D_MODEL
EPS = 1e-6
A_WIDTH = MIX_WIDTH // 2
A_GROUPS = 4
A_GROUP_DIM = A_WIDTH // A_GROUPS
B_WIDTH = MIX_WIDTH // 2
B_HEADS = 4
B_HEAD_DIM = B_WIDTH // B_HEADS
ROPE_BASE = 10000.0
AB_IN = 2 * A_WIDTH + 4 * B_WIDTH
C_HEADS = 16
C_HEAD_DIM = MIX_WIDTH // C_HEADS

kernel_name = "hybrid_gmlp_retention_stickbreaking_macaron"


def rms_norm(x, g):
    xf = x.astype(jnp.float32)
    y = xf * lax.rsqrt(jnp.mean(xf * xf, axis=-1, keepdims=True) + EPS)
    return (y * g.astype(jnp.float32)).astype(x.dtype)


def layer_norm(x, g, b):
    xf = x.astype(jnp.float32)
    mu = jnp.mean(xf, axis=-1, keepdims=True)
    var = jnp.mean(jnp.square(xf - mu), axis=-1, keepdims=True)
    y = (xf - mu) * lax.rsqrt(var + EPS)
    return (y * g.astype(jnp.float32) + b.astype(jnp.float32)).astype(x.dtype)


def swiglu(x, w_gate, w_up, w_down):
    return (jax.nn.silu(x @ w_gate) * (x @ w_up)) @ w_down


def rotary(x, positions):
    half = x.shape[-1] // 2
    inv = ROPE_BASE ** (-jnp.arange(half, dtype=jnp.float32) / half)
    ang = positions.astype(jnp.float32)[:, None] * inv[None, :]
    cos = jnp.cos(ang)[None, :, None, :]
    sin = jnp.sin(ang)[None, :, None, :]
    xf = x.astype(jnp.float32)
    x1, x2 = xf[..., :half], xf[..., half:]
    return jnp.concatenate([x1 * cos - x2 * sin, x1 * sin + x2 * cos], axis=-1).astype(x.dtype)


def gmlp_mixer(a, v_g, v_b, w_s, b_s):
    bsz, s_len, _ = a.shape
    u, v = a[..., :A_WIDTH], a[..., A_WIDTH:]
    v = layer_norm(v, v_g, v_b)
    v = v.reshape(bsz, s_len // BLOCK, BLOCK, A_GROUPS, A_GROUP_DIM)
    causal = jnp.tril(jnp.ones((BLOCK, BLOCK), dtype=bool))
    w = jnp.where(causal[None], w_s, jnp.zeros_like(w_s))
    s = jnp.einsum('gts,bcsgd->bctgd', w, v) + b_s.T[None, None, :, :, None]
    return u * s.reshape(bsz, s_len, A_WIDTH)


def retention_mixer(q, k, v, g, norm_g):
    bsz, s_len, _ = q.shape
    n_chunks = s_len // BLOCK
    pos = jnp.arange(s_len)
    q = rotary(q.reshape(bsz, s_len, B_HEADS, B_HEAD_DIM), pos)
    k = rotary(k.reshape(bsz, s_len, B_HEADS, B_HEAD_DIM), pos) * (B_HEAD_DIM ** -0.5)
    v = v.reshape(bsz, s_len, B_HEADS, B_HEAD_DIM)

    def chunks(t):
        return t.reshape(bsz, n_chunks, BLOCK, B_HEADS, B_HEAD_DIM).transpose(1, 0, 3, 2, 4).astype(jnp.float32)

    log_gamma = jnp.log1p(-(2.0 ** (-5.0 - jnp.arange(B_HEADS, dtype=jnp.float32))))
    idx = jnp.arange(BLOCK, dtype=jnp.float32)
    diff = idx[:, None] - idx[None, :]
    decay = jnp.where(diff >= 0, jnp.exp(log_gamma[:, None, None] * jnp.maximum(diff, 0.0)), 0.0)
    xi = jnp.exp(log_gamma[:, None] * (idx + 1.0))[:, :, None]
    zeta = jnp.exp(log_gamma[:, None] * (BLOCK - 1.0 - idx))[:, :, None]
    gamma_c = jnp.exp(log_gamma * BLOCK)[:, None, None]

    def step(state, qkv):
        qc, kc, vc = qkv
        scores = jnp.einsum('bhnd,bhmd->bhnm', qc, kc) * decay
        inner = jnp.einsum('bhnm,bhme->bhne', scores, vc)
        cross = jnp.einsum('bhnd,bhde->bhne', qc * xi, state)
        state = gamma_c * state + jnp.einsum('bhmd,bhme->bhde', kc * zeta, vc)
        return state, inner + cross

    state0 = jnp.zeros((bsz, B_HEADS, B_HEAD_DIM, B_HEAD_DIM), jnp.float32)
    _, out = lax.scan(step, state0, (chunks(q), chunks(k), chunks(v)))
    out = out.transpose(1, 0, 3, 2, 4).reshape(bsz, s_len, B_HEADS, B_HEAD_DIM)
    out = rms_norm(out, norm_g.reshape(B_HEADS, B_HEAD_DIM))
    y = jax.nn.silu(g.astype(jnp.float32)) * out.reshape(bsz, s_len, B_WIDTH)
    return y.astype(g.dtype)


def stick_breaking_mixer(q, k, v):
    in_dtype = q.dtype
    bsz, s_len, _ = q.shape
    n_blocks = s_len // BLOCK

    def heads(t):
        return t.reshape(bsz, s_len, C_HEADS, C_HEAD_DIM).transpose(0, 2, 1, 3).astype(jnp.float32)

    q = heads(q) * (C_HEAD_DIM ** -0.5)
    k = heads(k)
    v = heads(v)
    idx = jnp.arange(BLOCK)
    outs = []
    for i in range(n_blocks):
        qi = q[:, :, i * BLOCK:(i + 1) * BLOCK]
        kb = k[:, :, :(i + 1) * BLOCK].reshape(bsz, C_HEADS, i + 1, BLOCK, C_HEAD_DIM).transpose(2, 0, 1, 3, 4)
        vb = v[:, :, :(i + 1) * BLOCK].reshape(bsz, C_HEADS, i + 1, BLOCK, C_HEAD_DIM).transpose(2, 0, 1, 3, 4)
        starts = jnp.arange(i + 1) * BLOCK
        t_pos = i * BLOCK + idx

        def step(carry, inp, qi=qi, t_pos=t_pos):
            acc, o = carry
            kj, vj, start = inp
            z = jnp.einsum('bhtd,bhsd->bhts', qi, kj)
            valid = (start + idx)[None, :] < t_pos[:, None]
            log_1m = jnp.where(valid, jax.nn.log_sigmoid(-z), 0.0)
            rev = lax.cumsum(log_1m, axis=3, reverse=True)
            log_a = jax.nn.log_sigmoid(z) + (rev - log_1m) + acc[..., None]
            a = jnp.where(valid, jnp.exp(log_a), 0.0)
            o = o + jnp.einsum('bhts,bhsd->bhtd', a, vj)
            return (acc + rev[..., 0], o), None

        init = (jnp.zeros((bsz, C_HEADS, BLOCK), jnp.float32),
                jnp.zeros((bsz, C_HEADS, BLOCK, C_HEAD_DIM), jnp.float32))
        (_, o), _ = lax.scan(step, init, (kb, vb, starts), reverse=True)
        outs.append(o)
    out = jnp.concatenate(outs, axis=2)
    return out.transpose(0, 2, 1, 3).reshape(bsz, s_len, MIX_WIDTH).astype(in_dtype)


def even_mixer(h, w_in, w_out, v_g, v_b, w_s, b_s, ret_g):
    p = h @ w_in
    a = jax.nn.gelu(p[..., :2 * A_WIDTH])
    q_r, k_r, v_r, g_r = jnp.split(p[..., 2 * A_WIDTH:], 4, axis=-1)
    y = jnp.concatenate([gmlp_mixer(a, v_g, v_b, w_s, b_s),
                         retention_mixer(q_r, k_r, v_r, g_r, ret_g)], axis=-1)
    return y @ w_out


def odd_mixer(h, w_qkv, w_out):
    q, k, v = jnp.split(h @ w_qkv, 3, axis=-1)
    return stick_breaking_mixer(q, k, v) @ w_out


def setup_inputs(seed: int = 0) -> dict:
    key = jax.random.key(seed)
    ks = jax.random.split(key, 16)
    f32 = jnp.float32
    nrm = lambda k, shape: jax.random.normal(k, shape, f32)
    return {
        "x": nrm(ks[0], (BATCH, SEQ, D_MODEL)),
        "norm_g": 1.0 + 0.05 * nrm(ks[1], (DEPTH, 6, D_MODEL)),
        "ffn_w_gate": nrm(ks[2], (DEPTH, 2, D_MODEL, D_FF)) * D_MODEL ** -0.5,
        "ffn_w_up": nrm(ks[3], (DEPTH, 2, D_MODEL, D_FF)) * D_MODEL ** -0.5,
        "ffn_w_down": nrm(ks[4], (DEPTH, 2, D_FF, D_MODEL)) * D_FF ** -0.5,
        "ab_w_in": nrm(ks[5], (N_EVEN, D_MODEL, AB_IN)) * D_MODEL ** -0.5,
        "ab_w_out": nrm(ks[6], (N_EVEN, MIX_WIDTH, D_MODEL)) * MIX_WIDTH ** -0.5,
        "gmlp_v_norm_g": 1.0 + 0.05 * nrm(ks[7], (N_EVEN, A_WIDTH)),
        "gmlp_v_norm_b": 0.02 * nrm(ks[8], (N_EVEN, A_WIDTH)),
        "gmlp_w_s": nrm(ks[9], (N_EVEN, A_GROUPS, BLOCK, BLOCK)) * BLOCK ** -0.5,
        "gmlp_b_s": 1.0 + 0.02 * nrm(ks[10], (N_EVEN, A_GROUPS, BLOCK)),
        "ret_norm_g": 1.0 + 0.05 * nrm(ks[11], (N_EVEN, B_WIDTH)),
        "sb_w_qkv": nrm(ks[12], (N_ODD, D_MODEL, 3 * MIX_WIDTH)) * D_MODEL ** -0.5,
        "sb_w_out": nrm(ks[13], (N_ODD, MIX_WIDTH, D_MODEL)) * MIX_WIDTH ** -0.5,
    }


def reference(x, norm_g, ffn_w_gate, ffn_w_up, ffn_w_down, ab_w_in, ab_w_out,
              gmlp_v_norm_g, gmlp_v_norm_b, gmlp_w_s, gmlp_b_s, ret_norm_g,
              sb_w_qkv, sb_w_out):
    for layer in range(DEPTH):
        g = norm_g[layer]
        f = swiglu(rms_norm(x, g[0]), ffn_w_gate[layer, 0], ffn_w_up[layer, 0], ffn_w_down[layer, 0])
        x = x + 0.5 * rms_norm(f, g[1])
        h = rms_norm(x, g[2])
        if layer % 2 == 0:
            e = layer // 2
            m = even_mixer(h, ab_w_in[e], ab_w_out[e], gmlp_v_norm_g[e], gmlp_v_norm_b[e],
                           gmlp_w_s[e], gmlp_b_s[e], ret_norm_g[e])
        else:
            o = layer // 2
            m = odd_mixer(h, sb_w_qkv[o], sb_w_out[o])
        x = x + rms_norm(m, g[3])
        f = swiglu(rms_norm(x, g[4]), ffn_w_gate[layer, 1], ffn_w_up[layer, 1], ffn_w_down[layer, 1])
        x = x + 0.5 * rms_norm(f, g[5])
    return x
```

```python
import functools

import jax
import jax.numpy as jnp
from jax import lax
from jax.experimental import pallas as pl
from jax.experimental.pallas import tpu as pltpu

F32 = jnp.float32
BF16 = jnp.bfloat16

EPS = 1e-6
CHUNK = 128
A_GROUPS = 4
B_HEADS = 4
HEAD_B = 128
C_HEADS = 16
HEAD_C = 64
ROPE_BASE = 10000.0
LANES = 128
VMEM_LIMIT = 56 * 1024 * 1024

TOKEN_TILE = 512


def _rms(x, g):
    ms = jnp.mean(x * x, axis=-1, keepdims=True)
    return x * lax.rsqrt(ms + EPS) * g


def _silu(x):
    return x / (1.0 + jnp.exp(-x))


def _gelu_tanh(x):
    c = 0.7978845608028654
    return 0.5 * x * (1.0 + jnp.tanh(c * (x + 0.044715 * (x * x * x))))


def _const_spec(shape):
    nd = len(shape)
    return pl.BlockSpec(shape, lambda *_: (0,) * nd, pipeline_mode=pl.Buffered(1))


def _params(*sem):
    return pltpu.CompilerParams(dimension_semantics=sem, vmem_limit_bytes=VMEM_LIMIT)


def _ffn_kernel(x_ref, gpre_ref, gpost_ref, wg_ref, wu_ref, wd_ref, o_ref):
    x = x_ref[...]
    xn = _rms(x, gpre_ref[...]).astype(BF16)
    g = jnp.dot(xn, wg_ref[...], preferred_element_type=F32)
    u = jnp.dot(xn, wu_ref[...], preferred_element_type=F32)
    h = (_silu(g) * u).astype(BF16)
    f = jnp.dot(h, wd_ref[...], preferred_element_type=F32)
    o_ref[...] = x + 0.5 * _rms(f, gpost_ref[...])


def _ffn(x2, g_pre, g_post, wg, wu, wd):
    n, d = x2.shape
    dff = wg.shape[1]
    tm = min(TOKEN_TILE, n)
    row = pl.BlockSpec((tm, d), lambda i: (i, 0))
    return pl.pallas_call(
        _ffn_kernel,
        grid=(n // tm,),
        in_specs=[row, _const_spec((1, d)), _const_spec((1, d)),
                  _const_spec((d, dff)), _const_spec((d, dff)), _const_spec((dff, d))],
        out_specs=row,
        out_shape=jax.ShapeDtypeStruct((n, d), F32),
        compiler_params=_params("arbitrary"),
        name="ffn",
    )(x2, g_pre.reshape(1, d), g_post.reshape(1, d), wg, wu, wd)


def _even_kernel(x_ref, g2_ref, g3_ref, win_ref, wout_ref, vg_ref, vb_ref, ws_ref, bs_ref,
                 retg_ref, cos_ref, sin_ref, decay_ref, xi_ref, zeta_ref, gc_ref,
                 o_ref, state_ref, y_ref, *, n_chunks):
    @pl.when(pl.program_id(1) == 0)
    def _():
        state_ref[...] = jnp.zeros_like(state_ref)

    aw = A_GROUPS * CHUNK
    bw = B_HEADS * HEAD_B
    x = x_ref[...]
    h = _rms(x, g2_ref[...]).astype(BF16)
    p = jnp.dot(h, win_ref[...], preferred_element_type=F32)
    a = _gelu_tanh(p[:, :2 * aw])
    u = a[:, :aw]
    v = a[:, aw:]
    mu = jnp.mean(v, axis=-1, keepdims=True)
    vc = v - mu
    var = jnp.mean(vc * vc, axis=-1, keepdims=True)
    vn = vc * lax.rsqrt(var + EPS) * vg_ref[...] + vb_ref[...]
    off = 2 * aw
    q = p[:, off:off + bw]
    k = p[:, off + bw:off + 2 * bw]
    vr = p[:, off + 2 * bw:off + 3 * bw]
    gr = p[:, off + 3 * bw:off + 4 * bw]

    row = lax.broadcasted_iota(jnp.int32, (CHUNK, CHUNK), 0)
    col = lax.broadcasted_iota(jnp.int32, (CHUNK, CHUNK), 1)
    causal = row >= col

    for c in range(n_chunks):
        r0, r1 = c * CHUNK, (c + 1) * CHUNK
        for g in range(A_GROUPS):
            l0, l1 = g * CHUNK, (g + 1) * CHUNK
            w = jnp.where(causal, ws_ref[g], 0.0).astype(BF16)
            s = jnp.dot(w, vn[r0:r1, l0:l1].astype(BF16), preferred_element_type=F32) + bs_ref[g]
            y_ref[r0:r1, l0:l1] = (u[r0:r1, l0:l1] * s).astype(BF16)
        cosc = cos_ref[r0:r1, :]
        sinc = sin_ref[r0:r1, :]
        for hd in range(B_HEADS):
            l0, l1 = hd * HEAD_B, (hd + 1) * HEAD_B
            qc = q[r0:r1, l0:l1]
            kc = k[r0:r1, l0:l1]
            qr = qc * cosc + pltpu.roll(qc, HEAD_B // 2, axis=1) * sinc
            kr = kc * cosc + pltpu.roll(kc, HEAD_B // 2, axis=1) * sinc
            vb = vr[r0:r1, l0:l1].astype(BF16)
            sc = lax.dot_general(qr.astype(BF16), kr.astype(BF16), (((1,), (1,)), ((), ())),
                                 preferred_element_type=F32) * decay_ref[hd]
            inner = jnp.dot(sc.astype(BF16), vb, preferred_element_type=F32)
            st = state_ref[hd]
            cross = jnp.dot((qr * xi_ref[hd]).astype(BF16), st.astype(BF16),
                            preferred_element_type=F32)
            kz_t = (kr * zeta_ref[hd]).T.astype(BF16)
            state_ref[hd] = st * gc_ref[hd] + jnp.dot(kz_t, vb, preferred_element_type=F32)
            out = inner + cross
            ms = jnp.mean(out * out, axis=-1, keepdims=True)
            on = out * lax.rsqrt(ms + EPS) * retg_ref[:, l0:l1]
            y_ref[r0:r1, aw + l0:aw + l1] = (_silu(gr[r0:r1, l0:l1]) * on).astype(BF16)

    m = jnp.dot(y_ref[...], wout_ref[...], preferred_element_type=F32)
    o_ref[...] = x + _rms(m, g3_ref[...])


def _retention_tables(seq):
    half = HEAD_B // 2
    inv = ROPE_BASE ** (-jnp.arange(half, dtype=F32) / half)
    ang = jnp.arange(seq, dtype=F32)[:, None] * inv[None, :]
    cos = jnp.cos(ang)
    sin = jnp.sin(ang)
    cos_t = jnp.concatenate([cos, cos], axis=-1)
    sin_t = jnp.concatenate([-sin, sin], axis=-1)
    scale = HEAD_B ** -0.5
    log_gamma = jnp.log1p(-(2.0 ** (-5.0 - jnp.arange(B_HEADS, dtype=F32))))
    idx = jnp.arange(CHUNK, dtype=F32)
    diff = idx[:, None] - idx[None, :]
    decay = jnp.where(diff >= 0, jnp.exp(log_gamma[:, None, None] * jnp.maximum(diff, 0.0)), 0.0)
    xi = jnp.exp(log_gamma[:, None] * (idx + 1.0))
    zeta = jnp.exp(log_gamma[:, None] * (CHUNK - 1.0 - idx))
    gamma_c = jnp.exp(log_gamma * CHUNK)
    rep = lambda t: jnp.broadcast_to(t[:, :, None], (B_HEADS, CHUNK, LANES))
    gc = jnp.broadcast_to(gamma_c[:, None, None], (B_HEADS, 1, LANES))
    return cos_t, sin_t, decay * scale, rep(xi), rep(zeta * scale), gc


def _even_mixer(x2, bsz, seq, g2, g3, w_in, w_out, v_g, v_b, w_s, b_s, ret_g):
    n, d = x2.shape
    tm = min(TOKEN_TILE, seq)
    tiles = seq // tm
    aw = A_GROUPS * CHUNK
    bw = B_HEADS * HEAD_B
    cos_t, sin_t, decay, xi, zeta, gc = _retention_tables(seq)
    bs_rep = jnp.broadcast_to(b_s[:, :, None], (A_GROUPS, CHUNK, LANES))
    row = pl.BlockSpec((tm, d), lambda b, i: (b * tiles + i, 0))
    tab = pl.BlockSpec((tm, LANES), lambda b, i: (i, 0))
    return pl.pallas_call(
        functools.partial(_even_kernel, n_chunks=tm // CHUNK),
        grid=(bsz, tiles),
        in_specs=[row, _const_spec((1, d)), _const_spec((1, d)),
                  _const_spec(w_in.shape), _const_spec(w_out.shape),
                  _const_spec((1, aw)), _const_spec((1, aw)),
                  _const_spec((A_GROUPS, CHUNK, CHUNK)), _const_spec((A_GROUPS, CHUNK, LANES)),
                  _const_spec((1, bw)), tab, tab,
                  _const_spec((B_HEADS, CHUNK, CHUNK)), _const_spec((B_HEADS, CHUNK, LANES)),
                  _const_spec((B_HEADS, CHUNK, LANES)), _const_spec((B_HEADS, 1, LANES))],
        out_specs=row,
        out_shape=jax.ShapeDtypeStruct((n, d), F32),
        scratch_shapes=[pltpu.VMEM((B_HEADS, HEAD_B, HEAD_B), F32),
                        pltpu.VMEM((tm, aw + bw), BF16)],
        compiler_params=_params("arbitrary", "arbitrary"),
        name="even_mixer",
    )(x2, g2.reshape(1, d), g3.reshape(1, d), w_in, w_out, v_g.reshape(1, aw), v_b.reshape(1, aw),
      w_s, bs_rep, ret_g.reshape(1, bw), cos_t, sin_t, decay, xi, zeta, gc)


def _qkv_kernel(x_ref, g2_ref, w_ref, q_ref, k_ref, v_ref):
    d = x_ref.shape[1]
    h = _rms(x_ref[...], g2_ref[...]).astype(BF16)
    p = jnp.dot(h, w_ref[...], preferred_element_type=F32)
    q_ref[...] = (p[:, :d] * (HEAD_C ** -0.5)).astype(BF16)
    k_ref[...] = p[:, d:2 * d].astype(BF16)
    v_ref[...] = p[:, 2 * d:].astype(BF16)


def _qkv(x2, g2, w_qkv):
    n, d = x2.shape
    tm = min(TOKEN_TILE, n)
    row = pl.BlockSpec((tm, d), lambda i: (i, 0))
    out = jax.ShapeDtypeStruct((n, d), BF16)
    return pl.pallas_call(
        _qkv_kernel,
        grid=(n // tm,),
        in_specs=[row, _const_spec((1, d)), _const_spec(w_qkv.shape)],
        out_specs=[row, row, row],
        out_shape=[out, out, out],
        compiler_params=_params("arbitrary"),
        name="qkv_proj",
    )(x2, g2.reshape(1, d), w_qkv)


def _sb_kernel(q_ref, k_ref, v_ref, o_ref):
    i = pl.program_id(2)
    qp = q_ref[0]
    lane = lax.broadcasted_iota(jnp.int32, (CHUNK, LANES), 1)
    row = lax.broadcasted_iota(jnp.int32, (CHUNK, CHUNK), 0)
    col = lax.broadcasted_iota(jnp.int32, (CHUNK, CHUNK), 1)
    first = lane < HEAD_C
    zero = jnp.zeros_like(qp)
    qh = (jnp.where(first, qp, zero), jnp.where(first, zero, qp))
    valid = col < row
    rhs = jnp.concatenate([(row > col).astype(BF16), jnp.ones((CHUNK, LANES), BF16)], axis=1)

    def block(j, carry, diag):
        kb = k_ref[0, pl.ds(pl.multiple_of(j * CHUNK, CHUNK), CHUNK), :]
        vb = v_ref[0, pl.ds(pl.multiple_of(j * CHUNK, CHUNK), CHUNK), :]
        new = []
        for hh in range(2):
            acc, o = carry[hh]
            z = lax.dot_general(qh[hh], kb, (((1,), (1,)), ((), ())), preferred_element_type=F32)
            sp = jnp.maximum(z, 0.0) + jnp.log(1.0 + jnp.exp(-jnp.abs(z)))
            spm = jnp.where(valid, sp, 0.0) if diag else sp
            hi = spm.astype(BF16)
            lo = (spm - hi.astype(F32)).astype(BF16)
            r = (jnp.dot(hi, rhs, preferred_element_type=F32)
                 + jnp.dot(lo, rhs, preferred_element_type=F32))
            a = jnp.exp(z - sp - r[:, :CHUNK] - acc)
            if diag:
                a = jnp.where(valid, a, 0.0)
            o = o + jnp.dot(a.astype(BF16), vb, preferred_element_type=F32)
            new.append((acc + r[:, CHUNK:], o))
        return tuple(new)

    zf = jnp.zeros((CHUNK, LANES), F32)
    carry = block(i, ((zf, zf), (zf, zf)), True)
    carry = lax.fori_loop(0, i, lambda t, c: block(i - 1 - t, c, False), carry)
    o_ref[0] = jnp.where(first, carry[0][1], carry[1][1]).astype(BF16)


def _stick_breaking(q, k, v, bsz, seq):
    d = q.shape[-1]
    pairs = d // LANES
    q3, k3, v3 = (t.reshape(bsz, seq, d) for t in (q, k, v))
    qspec = pl.BlockSpec((1, CHUNK, LANES), lambda b, p, i: (b, i, p))
    kvspec = pl.BlockSpec((1, seq, LANES), lambda b, p, i: (b, 0, p))
    out = pl.pallas_call(
        _sb_kernel,
        grid=(bsz, pairs, seq // CHUNK),
        in_specs=[qspec, kvspec, kvspec],
        out_specs=qspec,
        out_shape=jax.ShapeDtypeStruct((bsz, seq, d), BF16),
        compiler_params=_params("arbitrary", "arbitrary", "arbitrary"),
        name="stick_breaking",
    )(q3, k3, v3)
    return out.reshape(bsz * seq, d)


def _outproj_kernel(x_ref, a_ref, w_ref, g_ref, o_ref):
    m = jnp.dot(a_ref[...], w_ref[...], preferred_element_type=F32)
    o_ref[...] = x_ref[...] + _rms(m, g_ref[...])


def _outproj(x2, a2, w_out, g3):
    n, d = x2.shape
    tm = min(TOKEN_TILE, n)
    row = pl.BlockSpec((tm, d), lambda i: (i, 0))
    return pl.pallas_call(
        _outproj_kernel,
        grid=(n // tm,),
        in_specs=[row, row, _const_spec(w_out.shape), _const_spec((1, d))],
        out_specs=row,
        out_shape=jax.ShapeDtypeStruct((n, d), F32),
        compiler_params=_params("arbitrary"),
        name="out_proj",
    )(x2, a2, w_out, g3.reshape(1, d))


def kernel(x, norm_g, ffn_w_gate, ffn_w_up, ffn_w_down, ab_w_in, ab_w_out, gmlp_v_norm_g,
           gmlp_v_norm_b, gmlp_w_s, gmlp_b_s, ret_norm_g, sb_w_qkv, sb_w_out):
    bsz, seq, d = x.shape
    depth = norm_g.shape[0]
    x2 = x.reshape(bsz * seq, d)
    wg, wu, wd = (w.astype(BF16) for w in (ffn_w_gate, ffn_w_up, ffn_w_down))
    for layer in range(depth):
        g = norm_g[layer]
        x2 = _ffn(x2, g[0], g[1], wg[layer, 0], wu[layer, 0], wd[layer, 0])
        if layer % 2 == 0:
            e = layer // 2
            x2 = _even_mixer(x2, bsz, seq, g[2], g[3], ab_w_in[e].astype(BF16),
                             ab_w_out[e].astype(BF16), gmlp_v_norm_g[e], gmlp_v_norm_b[e],
                             gmlp_w_s[e], gmlp_b_s[e], ret_norm_g[e])
        else:
            o = layer // 2
            q, k, v = _qkv(x2, g[2], sb_w_qkv[o].astype(BF16))
            att = _stick_breaking(q, k, v, bsz, seq)
            x2 = _outproj(x2, att, sb_w_out[o].astype(BF16), g[3])
        x2 = _ffn(x2, g[4], g[5], wg[layer, 1], wu[layer, 1], wd[layer, 1])
    return x2.reshape(bsz, seq, d)
```

```python
import functools

import jax
import jax.numpy as jnp
from jax import lax
from jax.experimental import pallas as pl
from jax.experimental.pallas import tpu as pltpu

F32 = jnp.float32
BF16 = jnp.bfloat16

EPS = 1e-6
CHUNK = 128
A_GROUPS = 4
B_HEADS = 4
HEAD_B = 128
C_HEADS = 16
HEAD_C = 64
ROPE_BASE = 10000.0
LANES = 128
VMEM_LIMIT = 56 * 1024 * 1024

TOKEN_TILE = 512


def _rms(x, g):
    ms = jnp.mean(x * x, axis=-1, keepdims=True)
    return x * lax.rsqrt(ms + EPS) * g


def _silu(x):
    return x / (1.0 + jnp.exp(-x))


def _gelu_tanh(x):
    c = 0.7978845608028654
    return 0.5 * x * (1.0 + jnp.tanh(c * (x + 0.044715 * (x * x * x))))


def _const_spec(shape):
    nd = len(shape)
    return pl.BlockSpec(shape, lambda *_: (0,) * nd, pipeline_mode=pl.Buffered(1))


def _params(*sem):
    return pltpu.CompilerParams(dimension_semantics=sem, vmem_limit_bytes=VMEM_LIMIT)


def _ffn_kernel(x_ref, gpre_ref, gpost_ref, wg_ref, wu_ref, wd_ref, o_ref):
    x = x_ref[...]
    xn = _rms(x, gpre_ref[...]).astype(BF16)
    g = jnp.dot(xn, wg_ref[...], preferred_element_type=F32)
    u = jnp.dot(xn, wu_ref[...], preferred_element_type=F32)
    h = (_silu(g) * u).astype(BF16)
    f = jnp.dot(h, wd_ref[...], preferred_element_type=F32)
    o_ref[...] = x + 0.5 * _rms(f, gpost_ref[...])


def _ffn(x2, g_pre, g_post, wg, wu, wd):
    n, d = x2.shape
    dff = wg.shape[1]
    tm = min(TOKEN_TILE, n)
    row = pl.BlockSpec((tm, d), lambda i: (i, 0))
    return pl.pallas_call(
        _ffn_kernel,
        grid=(n // tm,),
        in_specs=[row, _const_spec((1, d)), _const_spec((1, d)),
                  _const_spec((d, dff)), _const_spec((d, dff)), _const_spec((dff, d))],
        out_specs=row,
        out_shape=jax.ShapeDtypeStruct((n, d), F32),
        compiler_params=_params("arbitrary"),
        name="ffn",
    )(x2, g_pre.reshape(1, d), g_post.reshape(1, d), wg, wu, wd)


def _even_kernel(x_ref, g2_ref, g3_ref, win_ref, wout_ref, vg_ref, vb_ref, ws_ref, bs_ref,
                 retg_ref, cos_ref, sin_ref, decay_ref, xi_ref, zeta_ref, gc_ref,
                 o_ref, state_ref, y_ref, *, n_chunks):
    @pl.when(pl.program_id(1) == 0)
    def _():
        state_ref[...] = jnp.zeros_like(state_ref)

    aw = A_GROUPS * CHUNK
    bw = B_HEADS * HEAD_B
    x = x_ref[...]
    h = _rms(x, g2_ref[...]).astype(BF16)
    p = jnp.dot(h, win_ref[...], preferred_element_type=F32)
    a = _gelu_tanh(p[:, :2 * aw])
    u = a[:, :aw]
    v = a[:, aw:]
    mu = jnp.mean(v, axis=-1, keepdims=True)
    vc = v - mu
    var = jnp.mean(vc * vc, axis=-1, keepdims=True)
    vn = vc * lax.rsqrt(var + EPS) * vg_ref[...] + vb_ref[...]
    off = 2 * aw
    q = p[:, off:off + bw]
    k = p[:, off + bw:off + 2 * bw]
    vr = p[:, off + 2 * bw:off + 3 * bw]
    gr = p[:, off + 3 * bw:off + 4 * bw]

    row = lax.broadcasted_iota(jnp.int32, (CHUNK, CHUNK), 0)
    col = lax.broadcasted_iota(jnp.int32, (CHUNK, CHUNK), 1)
    causal = row >= col

    for c in range(n_chunks):
        r0, r1 = c * CHUNK, (c + 1) * CHUNK
        for g in range(A_GROUPS):
            l0, l1 = g * CHUNK, (g + 1) * CHUNK
            w = jnp.where(causal, ws_ref[g], 0.0).astype(BF16)
            s = jnp.dot(w, vn[r0:r1, l0:l1].astype(BF16), preferred_element_type=F32) + bs_ref[g]
            y_ref[r0:r1, l0:l1] = (u[r0:r1, l0:l1] * s).astype(BF16)
        cosc = cos_ref[r0:r1, :]
        sinc = sin_ref[r0:r1, :]
        for hd in range(B_HEADS):
            l0, l1 = hd * HEAD_B, (hd + 1) * HEAD_B
            qc = q[r0:r1, l0:l1]
            kc = k[r0:r1, l0:l1]
            qr = qc * cosc + pltpu.roll(qc, HEAD_B // 2, axis=1) * sinc
            kr = kc * cosc + pltpu.roll(kc, HEAD_B // 2, axis=1) * sinc
            vb = vr[r0:r1, l0:l1].astype(BF16)
            sc = lax.dot_general(qr.astype(BF16), kr.astype(BF16), (((1,), (1,)), ((), ())),
                                 preferred_element_type=F32) * decay_ref[hd]
            inner = jnp.dot(sc.astype(BF16), vb, preferred_element_type=F32)
            st = state_ref[hd]
            cross = jnp.dot((qr * xi_ref[hd]).astype(BF16), st.astype(BF16),
                            preferred_element_type=F32)
            kz_t = (kr * zeta_ref[hd]).T.astype(BF16)
            state_ref[hd] = st * gc_ref[hd] + jnp.dot(kz_t, vb, preferred_element_type=F32)
            out = inner + cross
            ms = jnp.mean(out * out, axis=-1, keepdims=True)
            on = out * lax.rsqrt(ms + EPS) * retg_ref[:, l0:l1]
            y_ref[r0:r1, aw + l0:aw + l1] = (_silu(gr[r0:r1, l0:l1]) * on).astype(BF16)

    m = jnp.dot(y_ref[...], wout_ref[...], preferred_element_type=F32)
    o_ref[...] = x + _rms(m, g3_ref[...])


def _retention_tables(seq):
    half = HEAD_B // 2
    inv = ROPE_BASE ** (-jnp.arange(half, dtype=F32) / half)
    ang = jnp.arange(seq, dtype=F32)[:, None] * inv[None, :]
    cos = jnp.cos(ang)
    sin = jnp.sin(ang)
    cos_t = jnp.concatenate([cos, cos], axis=-1)
    sin_t = jnp.concatenate([-sin, sin], axis=-1)
    scale = HEAD_B ** -0.5
    log_gamma = jnp.log1p(-(2.0 ** (-5.0 - jnp.arange(B_HEADS, dtype=F32))))
    idx = jnp.arange(CHUNK, dtype=F32)
    diff = idx[:, None] - idx[None, :]
    decay = jnp.where(diff >= 0, jnp.exp(log_gamma[:, None, None] * jnp.maximum(diff, 0.0)), 0.0)
    xi = jnp.exp(log_gamma[:, None] * (idx + 1.0))
    zeta = jnp.exp(log_gamma[:, None] * (CHUNK - 1.0 - idx))
    gamma_c = jnp.exp(log_gamma * CHUNK)
    rep = lambda t: jnp.broadcast_to(t[:, :, None], (B_HEADS, CHUNK, LANES))
    gc = jnp.broadcast_to(gamma_c[:, None, None], (B_HEADS, 1, LANES))
    return cos_t, sin_t, decay * scale, rep(xi), rep(zeta * scale), gc


def _even_mixer(x2, bsz, seq, g2, g3, w_in, w_out, v_g, v_b, w_s, b_s, ret_g):
    n, d = x2.shape
    tm = min(TOKEN_TILE, seq)
    tiles = seq // tm
    aw = A_GROUPS * CHUNK
    bw = B_HEADS * HEAD_B
    cos_t, sin_t, decay, xi, zeta, gc = _retention_tables(seq)
    bs_rep = jnp.broadcast_to(b_s[:, :, None], (A_GROUPS, CHUNK, LANES))
    row = pl.BlockSpec((tm, d), lambda b, i: (b * tiles + i, 0))
    tab = pl.BlockSpec((tm, LANES), lambda b, i: (i, 0))
    return pl.pallas_call(
        functools.partial(_even_kernel, n_chunks=tm // CHUNK),
        grid=(bsz, tiles),
        in_specs=[row, _const_spec((1, d)), _const_spec((1, d)),
                  _const_spec(w_in.shape), _const_spec(w_out.shape),
                  _const_spec((1, aw)), _const_spec((1, aw)),
                  _const_spec((A_GROUPS, CHUNK, CHUNK)), _const_spec((A_GROUPS, CHUNK, LANES)),
                  _const_spec((1, bw)), tab, tab,
                  _const_spec((B_HEADS, CHUNK, CHUNK)), _const_spec((B_HEADS, CHUNK, LANES)),
                  _const_spec((B_HEADS, CHUNK, LANES)), _const_spec((B_HEADS, 1, LANES))],
        out_specs=row,
        out_shape=jax.ShapeDtypeStruct((n, d), F32),
        scratch_shapes=[pltpu.VMEM((B_HEADS, HEAD_B, HEAD_B), F32),
                        pltpu.VMEM((tm, aw + bw), BF16)],
        compiler_params=_params("arbitrary", "arbitrary"),
        name="even_mixer",
    )(x2, g2.reshape(1, d), g3.reshape(1, d), w_in, w_out, v_g.reshape(1, aw), v_b.reshape(1, aw),
      w_s, bs_rep, ret_g.reshape(1, bw), cos_t, sin_t, decay, xi, zeta, gc)


def _qkv_kernel(x_ref, g2_ref, w_ref, q_ref, k_ref, v_ref):
    d = x_ref.shape[1]
    h = _rms(x_ref[...], g2_ref[...]).astype(BF16)
    p = jnp.dot(h, w_ref[...], preferred_element_type=F32)
    q_ref[...] = (p[:, :d] * (HEAD_C ** -0.5)).astype(BF16)
    k_ref[...] = p[:, d:2 * d].astype(BF16)
    v_ref[...] = p[:, 2 * d:].astype(BF16)


def _qkv(x2, g2, w_qkv):
    n, d = x2.shape
    tm = min(TOKEN_TILE, n)
    row = pl.BlockSpec((tm, d), lambda i: (i, 0))
    out = jax.ShapeDtypeStruct((n, d), BF16)
    return pl.pallas_call(
        _qkv_kernel,
        grid=(n // tm,),
        in_specs=[row, _const_spec((1, d)), _const_spec(w_qkv.shape)],
        out_specs=[row, row, row],
        out_shape=[out, out, out],
        compiler_params=_params("arbitrary"),
        name="qkv_proj",
    )(x2, g2.reshape(1, d), w_qkv)


SKIP_LOG = 88.0


def _softplus(z):
    return jnp.maximum(z, 0.0) + jnp.log(1.0 + jnp.exp(-jnp.abs(z)))


def _hi_lo(x):
    hi = x.astype(BF16)
    lo = (x - hi.astype(F32)).astype(BF16)
    return jnp.concatenate([hi, lo], axis=1)


def _sb_kernel(q_ref, k_ref, v_ref, tri_ref, rhs_ref, o_ref, *, group):
    seq = q_ref.shape[1]
    nblk = seq // CHUNK
    win = 2 * CHUNK
    lane = lax.broadcasted_iota(jnp.int32, (CHUNK, LANES), 1)
    first = lane < HEAD_C
    head_mask = (first, jnp.logical_not(first))
    col_minus_row = (lax.broadcasted_iota(jnp.int32, (CHUNK, win), 1)
                     - lax.broadcasted_iota(jnp.int32, (CHUNK, win), 0))

    def rows(blk):
        return blk * CHUNK if isinstance(blk, int) else pl.multiple_of(blk * CHUNK, CHUNK)

    def load_q(i):
        qp = q_ref[0, pl.ds(rows(i), CHUNK), :]
        zero = jnp.zeros_like(qp)
        return tuple(jnp.where(m, qp, zero) for m in head_mask)

    def window(i, off):
        w0 = rows(max(i - 1, 0) if isinstance(i, int) else i - 1)
        kwin = k_ref[0, pl.ds(w0, win), :]
        vwin = v_ref[0, pl.ds(w0, win), :]
        valid = col_minus_row < off
        out = []
        for qh in load_q(i):
            z = lax.dot_general(qh, kwin, (((1,), (1,)), ((), ())), preferred_element_type=F32)
            sp = _softplus(z)
            spm = jnp.where(valid, sp, 0.0)
            later = jnp.dot(_hi_lo(spm), tri_ref[...], preferred_element_type=F32)
            a = jnp.where(valid, jnp.exp(z - sp - later), 0.0)
            o = jnp.dot(a.astype(BF16), vwin, preferred_element_type=F32)
            out.append((jnp.sum(spm, axis=1, keepdims=True), o))
        return out

    def tail(i, state):
        qhs = load_q(i)

        def lowest(a0, a1):
            return jnp.min(jnp.minimum(a0, a1))

        def cond(c):
            return jnp.logical_and(c[0] >= 0, c[1] < SKIP_LOG)

        def body(c):
            j = c[0]
            kb = k_ref[0, pl.ds(rows(j), CHUNK), :]
            vb = v_ref[0, pl.ds(rows(j), CHUNK), :]
            new = []
            for hh in range(2):
                acc, o = c[2 + 2 * hh], c[3 + 2 * hh]
                z = lax.dot_general(qhs[hh], kb, (((1,), (1,)), ((), ())), preferred_element_type=F32)
                sp = _softplus(z)
                r = jnp.dot(_hi_lo(sp), rhs_ref[...], preferred_element_type=F32)
                a = jnp.exp(z - sp - r[:, :CHUNK] - acc)
                new += [acc + r[:, CHUNK:], o + jnp.dot(a.astype(BF16), vb, preferred_element_type=F32)]
            return (j - 1, lowest(new[0], new[2]), *new)

        acc0 = jnp.broadcast_to(state[0][0], (CHUNK, LANES))
        acc1 = jnp.broadcast_to(state[1][0], (CHUNK, LANES))
        c = lax.while_loop(cond, body, (i - 2, lowest(acc0, acc1), acc0, state[0][1], acc1, state[1][1]))
        return c[3], c[5]

    def store(i, o0, o1):
        o_ref[0, pl.ds(rows(i), CHUNK), :] = jnp.where(first, o0, o1).astype(BF16)

    def do_group(blocks, offs):
        states = [window(i, off) for i, off in zip(blocks, offs)]
        for i, st in zip(blocks, states):
            if isinstance(i, int) and i < 2:
                store(i, st[0][1], st[1][1])
            else:
                store(i, *tail(i, st))

    do_group(list(range(group)), [0] + [CHUNK] * (group - 1))

    def loop_body(n, carry):
        do_group([n * group + t for t in range(group)], [CHUNK] * group)
        return carry

    lax.fori_loop(1, nblk // group, loop_body, 0)


def _stick_breaking(q, k, v, bsz, seq):
    d = q.shape[-1]
    pairs = d // LANES
    group = 2
    assert (seq // CHUNK) % group == 0
    q3, k3, v3 = (t.reshape(bsz, seq, d) for t in (q, k, v))
    win = 2 * CHUNK
    idx = jnp.arange(win)
    tri = (idx[:, None] > idx[None, :]).astype(BF16)
    tri2 = jnp.concatenate([tri, tri], axis=0)
    rhs = jnp.concatenate([tri[:CHUNK, :CHUNK], jnp.ones((CHUNK, LANES), BF16)], axis=1)
    rhs2 = jnp.concatenate([rhs, rhs], axis=0)
    spec = pl.BlockSpec((1, seq, LANES), lambda b, p: (b, 0, p))
    out = pl.pallas_call(
        functools.partial(_sb_kernel, group=group),
        grid=(bsz, pairs),
        in_specs=[spec, spec, spec, _const_spec(tri2.shape), _const_spec(rhs2.shape)],
        out_specs=spec,
        out_shape=jax.ShapeDtypeStruct((bsz, seq, d), BF16),
        compiler_params=_params("arbitrary", "arbitrary"),
        name="stick_breaking",
    )(q3, k3, v3, tri2, rhs2)
    return out.reshape(bsz * seq, d)


def _outproj_kernel(x_ref, a_ref, w_ref, g_ref, o_ref):
    m = jnp.dot(a_ref[...], w_ref[...], preferred_element_type=F32)
    o_ref[...] = x_ref[...] + _rms(m, g_ref[...])


def _outproj(x2, a2, w_out, g3):
    n, d = x2.shape
    tm = min(TOKEN_TILE, n)
    row = pl.BlockSpec((tm, d), lambda i: (i, 0))
    return pl.pallas_call(
        _outproj_kernel,
        grid=(n // tm,),
        in_specs=[row, row, _const_spec(w_out.shape), _const_spec((1, d))],
        out_specs=row,
        out_shape=jax.ShapeDtypeStruct((n, d), F32),
        compiler_params=_params("arbitrary"),
        name="out_proj",
    )(x2, a2, w_out, g3.reshape(1, d))


def kernel(x, norm_g, ffn_w_gate, ffn_w_up, ffn_w_down, ab_w_in, ab_w_out, gmlp_v_norm_g,
           gmlp_v_norm_b, gmlp_w_s, gmlp_b_s, ret_norm_g, sb_w_qkv, sb_w_out):
    bsz, seq, d = x.shape
    depth = norm_g.shape[0]
    x2 = x.reshape(bsz * seq, d)
    wg, wu, wd = (w.astype(BF16) for w in (ffn_w_gate, ffn_w_up, ffn_w_down))
    for layer in range(depth):
        g = norm_g[layer]
        x2 = _ffn(x2, g[0], g[1], wg[layer, 0], wu[layer, 0], wd[layer, 0])
        if layer % 2 == 0:
            e = layer // 2
            x2 = _even_mixer(x2, bsz, seq, g[2], g[3], ab_w_in[e].astype(BF16),
                             ab_w_out[e].astype(BF16), gmlp_v_norm_g[e], gmlp_v_norm_b[e],
                             gmlp_w_s[e], gmlp_b_s[e], ret_norm_g[e])
        else:
            o = layer // 2
            q, k, v = _qkv(x2, g[2], sb_w_qkv[o].astype(BF16))
            att = _stick_breaking(q, k, v, bsz, seq)
            x2 = _outproj(x2, att, sb_w_out[o].astype(BF16), g[3])
        x2 = _ffn(x2, g[4], g[5], wg[layer, 1], wu[layer, 1], wd[layer, 1])
    return x2.reshape(bsz, seq, d)
```

```python
import functools

import jax
import jax.numpy as jnp
from jax import lax
from jax.experimental import pallas as pl
from jax.experimental.pallas import tpu as pltpu

F32 = jnp.float32
BF16 = jnp.bfloat16

EPS = 1e-6
CHUNK = 128
A_GROUPS = 4
B_HEADS = 4
HEAD_B = 128
C_HEADS = 16
HEAD_C = 64
ROPE_BASE = 10000.0
LANES = 128
VMEM_LIMIT = 56 * 1024 * 1024

TOKEN_TILE = 512


def _rms(x, g):
    ms = jnp.mean(x * x, axis=-1, keepdims=True)
    return x * lax.rsqrt(ms + EPS) * g


def _silu(x):
    return x / (1.0 + jnp.exp(-x))


def _gelu_tanh(x):
    c = 0.7978845608028654
    return 0.5 * x * (1.0 + jnp.tanh(c * (x + 0.044715 * (x * x * x))))


def _const_spec(shape):
    nd = len(shape)
    return pl.BlockSpec(shape, lambda *_: (0,) * nd, pipeline_mode=pl.Buffered(1))


def _params(*sem):
    return pltpu.CompilerParams(dimension_semantics=sem, vmem_limit_bytes=VMEM_LIMIT)


def _ffn_kernel(x_ref, gpre_ref, gpost_ref, wg_ref, wu_ref, wd_ref, o_ref):
    x = x_ref[...]
    xn = _rms(x, gpre_ref[...]).astype(BF16)
    g = jnp.dot(xn, wg_ref[...], preferred_element_type=F32)
    u = jnp.dot(xn, wu_ref[...], preferred_element_type=F32)
    h = (_silu(g) * u).astype(BF16)
    f = jnp.dot(h, wd_ref[...], preferred_element_type=F32)
    o_ref[...] = x + 0.5 * _rms(f, gpost_ref[...])


def _ffn(x2, g_pre, g_post, wg, wu, wd):
    n, d = x2.shape
    dff = wg.shape[1]
    tm = min(TOKEN_TILE, n)
    row = pl.BlockSpec((tm, d), lambda i: (i, 0))
    return pl.pallas_call(
        _ffn_kernel,
        grid=(n // tm,),
        in_specs=[row, _const_spec((1, d)), _const_spec((1, d)),
                  _const_spec((d, dff)), _const_spec((d, dff)), _const_spec((dff, d))],
        out_specs=row,
        out_shape=jax.ShapeDtypeStruct((n, d), F32),
        compiler_params=_params("arbitrary"),
        name="ffn",
    )(x2, g_pre.reshape(1, d), g_post.reshape(1, d), wg, wu, wd)


def _even_kernel(x_ref, g2_ref, g3_ref, win_ref, wout_ref, vg_ref, vb_ref, ws_ref, bs_ref,
                 retg_ref, cos_ref, sin_ref, decay_ref, xi_ref, zeta_ref, gc_ref,
                 o_ref, state_ref, y_ref, *, n_chunks):
    @pl.when(pl.program_id(1) == 0)
    def _():
        state_ref[...] = jnp.zeros_like(state_ref)

    aw = A_GROUPS * CHUNK
    bw = B_HEADS * HEAD_B
    x = x_ref[...]
    h = _rms(x, g2_ref[...]).astype(BF16)
    p = jnp.dot(h, win_ref[...], preferred_element_type=F32)
    a = _gelu_tanh(p[:, :2 * aw])
    u = a[:, :aw]
    v = a[:, aw:]
    mu = jnp.mean(v, axis=-1, keepdims=True)
    vc = v - mu
    var = jnp.mean(vc * vc, axis=-1, keepdims=True)
    vn = vc * lax.rsqrt(var + EPS) * vg_ref[...] + vb_ref[...]
    off = 2 * aw
    q = p[:, off:off + bw]
    k = p[:, off + bw:off + 2 * bw]
    vr = p[:, off + 2 * bw:off + 3 * bw]
    gr = p[:, off + 3 * bw:off + 4 * bw]

    row = lax.broadcasted_iota(jnp.int32, (CHUNK, CHUNK), 0)
    col = lax.broadcasted_iota(jnp.int32, (CHUNK, CHUNK), 1)
    causal = row >= col

    for c in range(n_chunks):
        r0, r1 = c * CHUNK, (c + 1) * CHUNK
        for g in range(A_GROUPS):
            l0, l1 = g * CHUNK, (g + 1) * CHUNK
            w = jnp.where(causal, ws_ref[g], 0.0).astype(BF16)
            s = jnp.dot(w, vn[r0:r1, l0:l1].astype(BF16), preferred_element_type=F32) + bs_ref[g]
            y_ref[r0:r1, l0:l1] = (u[r0:r1, l0:l1] * s).astype(BF16)
        cosc = cos_ref[r0:r1, :]
        sinc = sin_ref[r0:r1, :]
        for hd in range(B_HEADS):
            l0, l1 = hd * HEAD_B, (hd + 1) * HEAD_B
            qc = q[r0:r1, l0:l1]
            kc = k[r0:r1, l0:l1]
            qr = qc * cosc + pltpu.roll(qc, HEAD_B // 2, axis=1) * sinc
            kr = kc * cosc + pltpu.roll(kc, HEAD_B // 2, axis=1) * sinc
            vb = vr[r0:r1, l0:l1].astype(BF16)
            sc = lax.dot_general(qr.astype(BF16), kr.astype(BF16), (((1,), (1,)), ((), ())),
                                 preferred_element_type=F32) * decay_ref[hd]
            inner = jnp.dot(sc.astype(BF16), vb, preferred_element_type=F32)
            st = state_ref[hd]
            cross = jnp.dot((qr * xi_ref[hd]).astype(BF16), st.astype(BF16),
                            preferred_element_type=F32)
            kz_t = (kr * zeta_ref[hd]).T.astype(BF16)
            state_ref[hd] = st * gc_ref[hd] + jnp.dot(kz_t, vb, preferred_element_type=F32)
            out = inner + cross
            ms = jnp.mean(out * out, axis=-1, keepdims=True)
            on = out * lax.rsqrt(ms + EPS) * retg_ref[:, l0:l1]
            y_ref[r0:r1, aw + l0:aw + l1] = (_silu(gr[r0:r1, l0:l1]) * on).astype(BF16)

    m = jnp.dot(y_ref[...], wout_ref[...], preferred_element_type=F32)
    o_ref[...] = x + _rms(m, g3_ref[...])


def _retention_tables(seq):
    half = HEAD_B // 2
    inv = ROPE_BASE ** (-jnp.arange(half, dtype=F32) / half)
    ang = jnp.arange(seq, dtype=F32)[:, None] * inv[None, :]
    cos = jnp.cos(ang)
    sin = jnp.sin(ang)
    cos_t = jnp.concatenate([cos, cos], axis=-1)
    sin_t = jnp.concatenate([-sin, sin], axis=-1)
    scale = HEAD_B ** -0.5
    log_gamma = jnp.log1p(-(2.0 ** (-5.0 - jnp.arange(B_HEADS, dtype=F32))))
    idx = jnp.arange(CHUNK, dtype=F32)
    diff = idx[:, None] - idx[None, :]
    decay = jnp.where(diff >= 0, jnp.exp(log_gamma[:, None, None] * jnp.maximum(diff, 0.0)), 0.0)
    xi = jnp.exp(log_gamma[:, None] * (idx + 1.0))
    zeta = jnp.exp(log_gamma[:, None] * (CHUNK - 1.0 - idx))
    gamma_c = jnp.exp(log_gamma * CHUNK)
    rep = lambda t: jnp.broadcast_to(t[:, :, None], (B_HEADS, CHUNK, LANES))
    gc = jnp.broadcast_to(gamma_c[:, None, None], (B_HEADS, 1, LANES))
    return cos_t, sin_t, decay * scale, rep(xi), rep(zeta * scale), gc


def _even_mixer(x2, bsz, seq, g2, g3, w_in, w_out, v_g, v_b, w_s, b_s, ret_g):
    n, d = x2.shape
    tm = min(TOKEN_TILE, seq)
    tiles = seq // tm
    aw = A_GROUPS * CHUNK
    bw = B_HEADS * HEAD_B
    cos_t, sin_t, decay, xi, zeta, gc = _retention_tables(seq)
    bs_rep = jnp.broadcast_to(b_s[:, :, None], (A_GROUPS, CHUNK, LANES))
    row = pl.BlockSpec((tm, d), lambda b, i: (b * tiles + i, 0))
    tab = pl.BlockSpec((tm, LANES), lambda b, i: (i, 0))
    return pl.pallas_call(
        functools.partial(_even_kernel, n_chunks=tm // CHUNK),
        grid=(bsz, tiles),
        in_specs=[row, _const_spec((1, d)), _const_spec((1, d)),
                  _const_spec(w_in.shape), _const_spec(w_out.shape),
                  _const_spec((1, aw)), _const_spec((1, aw)),
                  _const_spec((A_GROUPS, CHUNK, CHUNK)), _const_spec((A_GROUPS, CHUNK, LANES)),
                  _const_spec((1, bw)), tab, tab,
                  _const_spec((B_HEADS, CHUNK, CHUNK)), _const_spec((B_HEADS, CHUNK, LANES)),
                  _const_spec((B_HEADS, CHUNK, LANES)), _const_spec((B_HEADS, 1, LANES))],
        out_specs=row,
        out_shape=jax.ShapeDtypeStruct((n, d), F32),
        scratch_shapes=[pltpu.VMEM((B_HEADS, HEAD_B, HEAD_B), F32),
                        pltpu.VMEM((tm, aw + bw), BF16)],
        compiler_params=_params("arbitrary", "arbitrary"),
        name="even_mixer",
    )(x2, g2.reshape(1, d), g3.reshape(1, d), w_in, w_out, v_g.reshape(1, aw), v_b.reshape(1, aw),
      w_s, bs_rep, ret_g.reshape(1, bw), cos_t, sin_t, decay, xi, zeta, gc)


def _qkv_kernel(x_ref, g2_ref, w_ref, q_ref, k_ref, v_ref):
    d = x_ref.shape[1]
    h = _rms(x_ref[...], g2_ref[...]).astype(BF16)
    p = jnp.dot(h, w_ref[...], preferred_element_type=F32)
    q_ref[...] = (p[:, :d] * (HEAD_C ** -0.5)).astype(BF16)
    k_ref[...] = p[:, d:2 * d].astype(BF16)
    v_ref[...] = p[:, 2 * d:].astype(BF16)


def _qkv(x2, g2, w_qkv):
    n, d = x2.shape
    tm = min(TOKEN_TILE, n)
    row = pl.BlockSpec((tm, d), lambda i: (i, 0))
    out = jax.ShapeDtypeStruct((n, d), BF16)
    return pl.pallas_call(
        _qkv_kernel,
        grid=(n // tm,),
        in_specs=[row, _const_spec((1, d)), _const_spec(w_qkv.shape)],
        out_specs=[row, row, row],
        out_shape=[out, out, out],
        compiler_params=_params("arbitrary"),
        name="qkv_proj",
    )(x2, g2.reshape(1, d), w_qkv)


SKIP_LOG = 88.0
WINDOW_BLOCKS = 3
MASKED_LOG = -1e30


def _softplus(z):
    return jnp.maximum(z, 0.0) + jnp.log(1.0 + jnp.exp(-jnp.abs(z)))


def _hi_lo(x):
    hi = x.astype(BF16)
    lo = (x - hi.astype(F32)).astype(BF16)
    return jnp.concatenate([hi, lo], axis=1)


def _sb_kernel(q_ref, k_ref, v_ref, rhs_ref, o_ref, ls_scr, hl_scr, a_scr, *, group):
    seq = q_ref.shape[1]
    nblk = seq // CHUNK
    win = WINDOW_BLOCKS * CHUNK
    pair_rows = 2 * CHUNK
    lane = lax.broadcasted_iota(jnp.int32, (CHUNK, LANES), 1)
    first = lane < HEAD_C
    col_minus_row = (lax.broadcasted_iota(jnp.int32, (CHUNK, win), 1)
                     - lax.broadcasted_iota(jnp.int32, (CHUNK, win), 0))
    contract_last = (((1,), (1,)), ((), ()))

    def rows(blk):
        return blk * CHUNK if isinstance(blk, int) else pl.multiple_of(blk * CHUNK, CHUNK)

    def load_q(i):
        qp = q_ref[0, pl.ds(rows(i), CHUNK), :]
        zero = jnp.zeros_like(qp)
        return jnp.concatenate([jnp.where(first, qp, zero), jnp.where(first, zero, qp)], axis=0)

    def windows(blocks, starts, offs):
        for g, (i, w0) in enumerate(zip(blocks, starts)):
            kwin = k_ref[0, pl.ds(w0, win), :]
            ls_scr[g * pair_rows:(g + 1) * pair_rows, :] = lax.dot_general(
                load_q(i), kwin, contract_last, preferred_element_type=F32)
        for g, off in enumerate(offs):
            sl = slice(g * pair_rows, (g + 1) * pair_rows)
            valid = (col_minus_row < off)[None]
            z = ls_scr[sl, :].reshape(2, CHUNK, win)
            sp = _softplus(z)
            spm = jnp.where(valid, sp, 0.0)
            ls_scr[sl, :] = jnp.where(valid, z - sp, MASKED_LOG).reshape(pair_rows, win)
            for b in range(WINDOW_BLOCKS):
                hl_scr[b, sl, :] = _hi_lo(spm[:, :, b * CHUNK:(b + 1) * CHUNK].reshape(pair_rows, CHUNK))
        r = [jnp.dot(hl_scr[b], rhs_ref[...], preferred_element_type=F32) for b in range(WINDOW_BLOCKS)]
        acc = None
        for b in reversed(range(WINDOW_BLOCKS)):
            later = r[b][:, :CHUNK] if acc is None else r[b][:, :CHUNK] + acc
            cols = slice(b * CHUNK, (b + 1) * CHUNK)
            a_scr[:, cols] = jnp.exp(ls_scr[:, cols] - later).astype(BF16)
            acc = r[b][:, CHUNK:] if acc is None else acc + r[b][:, CHUNK:]
        out = []
        for g, w0 in enumerate(starts):
            sl = slice(g * pair_rows, (g + 1) * pair_rows)
            vwin = v_ref[0, pl.ds(w0, win), :]
            o = jnp.dot(a_scr[sl, :], vwin, preferred_element_type=F32)
            a = acc[sl, :]
            out.append((a[:CHUNK], o[:CHUNK], a[CHUNK:], o[CHUNK:]))
        return out

    def tail(i, state):
        q2 = load_q(i)

        def lowest(a0, a1):
            return jnp.min(jnp.minimum(a0, a1))

        def cond(c):
            return jnp.logical_and(c[0] >= 0, c[1] < SKIP_LOG)

        def body(c):
            j = c[0]
            kb = k_ref[0, pl.ds(rows(j), CHUNK), :]
            vb = v_ref[0, pl.ds(rows(j), CHUNK), :]
            acc = jnp.concatenate([c[2], c[4]], axis=0)
            z = lax.dot_general(q2, kb, contract_last, preferred_element_type=F32)
            sp = _softplus(z)
            r = jnp.dot(_hi_lo(sp), rhs_ref[...], preferred_element_type=F32)
            a = jnp.exp(z - sp - r[:, :CHUNK] - acc)
            o = jnp.dot(a.astype(BF16), vb, preferred_element_type=F32)
            acc = acc + r[:, CHUNK:]
            return (j - 1, jnp.min(acc), acc[:CHUNK], c[3] + o[:CHUNK], acc[CHUNK:], c[5] + o[CHUNK:])

        c = lax.while_loop(cond, body, (i - WINDOW_BLOCKS, lowest(state[0], state[2]), *state))
        return c[3], c[5]

    def do_group(blocks, starts, offs):
        states = windows(blocks, starts, offs)
        for i, st in zip(blocks, states):
            if isinstance(i, int) and i < WINDOW_BLOCKS:
                o0, o1 = st[1], st[3]
            else:
                o0, o1 = tail(i, st)
            o_ref[0, pl.ds(rows(i), CHUNK), :] = jnp.where(first, o0, o1).astype(BF16)

    lead = list(range(group))
    lead_start = [max(i - WINDOW_BLOCKS + 1, 0) for i in lead]
    do_group(lead, [rows(s) for s in lead_start], [(i - s) * CHUNK for i, s in zip(lead, lead_start)])

    def loop_body(n, carry):
        blocks = [n * group + t for t in range(group)]
        do_group(blocks, [rows(i - WINDOW_BLOCKS + 1) for i in blocks], [win - CHUNK] * group)
        return carry

    lax.fori_loop(1, nblk // group, loop_body, 0)


def _stick_breaking(q, k, v, bsz, seq):
    d = q.shape[-1]
    pairs = d // LANES
    group = 4
    assert (seq // CHUNK) % group == 0 and group >= WINDOW_BLOCKS - 1 and seq >= WINDOW_BLOCKS * CHUNK
    q3, k3, v3 = (t.reshape(bsz, seq, d) for t in (q, k, v))
    idx = jnp.arange(CHUNK)
    later = (idx[:, None] > idx[None, :]).astype(BF16)
    rhs = jnp.concatenate([later, jnp.ones((CHUNK, LANES), BF16)], axis=1)
    rhs2 = jnp.concatenate([rhs, rhs], axis=0)
    grows = group * 2 * CHUNK
    win = WINDOW_BLOCKS * CHUNK
    spec = pl.BlockSpec((1, seq, LANES), lambda b, p: (b, 0, p))
    out = pl.pallas_call(
        functools.partial(_sb_kernel, group=group),
        grid=(bsz, pairs),
        in_specs=[spec, spec, spec, _const_spec(rhs2.shape)],
        out_specs=spec,
        out_shape=jax.ShapeDtypeStruct((bsz, seq, d), BF16),
        scratch_shapes=[pltpu.VMEM((grows, win), F32),
                        pltpu.VMEM((WINDOW_BLOCKS, grows, 2 * CHUNK), BF16),
                        pltpu.VMEM((grows, win), BF16)],
        compiler_params=_params("arbitrary", "arbitrary"),
        name="stick_breaking",
    )(q3, k3, v3, rhs2)
    return out.reshape(bsz * seq, d)


def _outproj_kernel(x_ref, a_ref, w_ref, g_ref, o_ref):
    m = jnp.dot(a_ref[...], w_ref[...], preferred_element_type=F32)
    o_ref[...] = x_ref[...] + _rms(m, g_ref[...])


def _outproj(x2, a2, w_out, g3):
    n, d = x2.shape
    tm = min(TOKEN_TILE, n)
    row = pl.BlockSpec((tm, d), lambda i: (i, 0))
    return pl.pallas_call(
        _outproj_kernel,
        grid=(n // tm,),
        in_specs=[row, row, _const_spec(w_out.shape), _const_spec((1, d))],
        out_specs=row,
        out_shape=jax.ShapeDtypeStruct((n, d), F32),
        compiler_params=_params("arbitrary"),
        name="out_proj",
    )(x2, a2, w_out, g3.reshape(1, d))


def kernel(x, norm_g, ffn_w_gate, ffn_w_up, ffn_w_down, ab_w_in, ab_w_out, gmlp_v_norm_g,
           gmlp_v_norm_b, gmlp_w_s, gmlp_b_s, ret_norm_g, sb_w_qkv, sb_w_out):
    bsz, seq, d = x.shape
    depth = norm_g.shape[0]
    x2 = x.reshape(bsz * seq, d)
    wg, wu, wd = (w.astype(BF16) for w in (ffn_w_gate, ffn_w_up, ffn_w_down))
    for layer in range(depth):
        g = norm_g[layer]
        x2 = _ffn(x2, g[0], g[1], wg[layer, 0], wu[layer, 0], wd[layer, 0])
        if layer % 2 == 0:
            e = layer // 2
            x2 = _even_mixer(x2, bsz, seq, g[2], g[3], ab_w_in[e].astype(BF16),
                             ab_w_out[e].astype(BF16), gmlp_v_norm_g[e], gmlp_v_norm_b[e],
                             gmlp_w_s[e], gmlp_b_s[e], ret_norm_g[e])
        else:
            o = layer // 2
            q, k, v = _qkv(x2, g[2], sb_w_qkv[o].astype(BF16))
            att = _stick_breaking(q, k, v, bsz, seq)
            x2 = _outproj(x2, att, sb_w_out[o].astype(BF16), g[3])
        x2 = _ffn(x2, g[4], g[5], wg[layer, 1], wu[layer, 1], wd[layer, 1])
    return x2.reshape(bsz, seq, d)
```

```python
import functools

import jax
import jax.numpy as jnp
from jax import lax
from jax.experimental import pallas as pl
from jax.experimental.pallas import tpu as pltpu

F32 = jnp.float32
BF16 = jnp.bfloat16

EPS = 1e-6
CHUNK = 128
A_GROUPS = 4
B_HEADS = 4
HEAD_B = 128
C_HEADS = 16
HEAD_C = 64
ROPE_BASE = 10000.0
LANES = 128
VMEM_LIMIT = 56 * 1024 * 1024

TOKEN_TILE = 512


def _rms(x, g):
    ms = jnp.mean(x * x, axis=-1, keepdims=True)
    return x * lax.rsqrt(ms + EPS) * g


def _silu(x):
    return x / (1.0 + jnp.exp(-x))


def _gelu_tanh(x):
    c = 0.7978845608028654
    return 0.5 * x * (1.0 + jnp.tanh(c * (x + 0.044715 * (x * x * x))))


def _const_spec(shape):
    nd = len(shape)
    return pl.BlockSpec(shape, lambda *_: (0,) * nd, pipeline_mode=pl.Buffered(1))


def _params(*sem):
    return pltpu.CompilerParams(dimension_semantics=sem, vmem_limit_bytes=VMEM_LIMIT)


def _ffn_kernel(x_ref, gpre_ref, gpost_ref, wg_ref, wu_ref, wd_ref, o_ref):
    x = x_ref[...]
    xn = _rms(x, gpre_ref[...]).astype(BF16)
    g = jnp.dot(xn, wg_ref[...], preferred_element_type=F32)
    u = jnp.dot(xn, wu_ref[...], preferred_element_type=F32)
    h = (_silu(g) * u).astype(BF16)
    f = jnp.dot(h, wd_ref[...], preferred_element_type=F32)
    o_ref[...] = x + 0.5 * _rms(f, gpost_ref[...])


def _ffn(x2, g_pre, g_post, wg, wu, wd):
    n, d = x2.shape
    dff = wg.shape[1]
    tm = min(TOKEN_TILE, n)
    row = pl.BlockSpec((tm, d), lambda i: (i, 0))
    return pl.pallas_call(
        _ffn_kernel,
        grid=(n // tm,),
        in_specs=[row, _const_spec((1, d)), _const_spec((1, d)),
                  _const_spec((d, dff)), _const_spec((d, dff)), _const_spec((dff, d))],
        out_specs=row,
        out_shape=jax.ShapeDtypeStruct((n, d), F32),
        compiler_params=_params("arbitrary"),
        name="ffn",
    )(x2, g_pre.reshape(1, d), g_post.reshape(1, d), wg, wu, wd)


def _even_kernel(x_ref, g2_ref, g3_ref, win_ref, wout_ref, vg_ref, vb_ref, ws_ref, bs_ref,
                 retg_ref, cos_ref, sin_ref, decay_ref, xi_ref, zeta_ref, gc_ref,
                 o_ref, state_ref, y_ref, *, n_chunks):
    @pl.when(pl.program_id(1) == 0)
    def _():
        state_ref[...] = jnp.zeros_like(state_ref)

    aw = A_GROUPS * CHUNK
    bw = B_HEADS * HEAD_B
    x = x_ref[...]
    h = _rms(x, g2_ref[...]).astype(BF16)
    p = jnp.dot(h, win_ref[...], preferred_element_type=F32)
    a = _gelu_tanh(p[:, :2 * aw])
    u = a[:, :aw]
    v = a[:, aw:]
    mu = jnp.mean(v, axis=-1, keepdims=True)
    vc = v - mu
    var = jnp.mean(vc * vc, axis=-1, keepdims=True)
    vn = vc * lax.rsqrt(var + EPS) * vg_ref[...] + vb_ref[...]
    off = 2 * aw
    q = p[:, off:off + bw]
    k = p[:, off + bw:off + 2 * bw]
    vr = p[:, off + 2 * bw:off + 3 * bw]
    gr = p[:, off + 3 * bw:off + 4 * bw]

    row = lax.broadcasted_iota(jnp.int32, (CHUNK, CHUNK), 0)
    col = lax.broadcasted_iota(jnp.int32, (CHUNK, CHUNK), 1)
    causal = row >= col

    for c in range(n_chunks):
        r0, r1 = c * CHUNK, (c + 1) * CHUNK
        for g in range(A_GROUPS):
            l0, l1 = g * CHUNK, (g + 1) * CHUNK
            w = jnp.where(causal, ws_ref[g], 0.0).astype(BF16)
            s = jnp.dot(w, vn[r0:r1, l0:l1].astype(BF16), preferred_element_type=F32) + bs_ref[g]
            y_ref[r0:r1, l0:l1] = (u[r0:r1, l0:l1] * s).astype(BF16)
        cosc = cos_ref[r0:r1, :]
        sinc = sin_ref[r0:r1, :]
        for hd in range(B_HEADS):
            l0, l1 = hd * HEAD_B, (hd + 1) * HEAD_B
            qc = q[r0:r1, l0:l1]
            kc = k[r0:r1, l0:l1]
            qr = qc * cosc + pltpu.roll(qc, HEAD_B // 2, axis=1) * sinc
            kr = kc * cosc + pltpu.roll(kc, HEAD_B // 2, axis=1) * sinc
            vb = vr[r0:r1, l0:l1].astype(BF16)
            sc = lax.dot_general(qr.astype(BF16), kr.astype(BF16), (((1,), (1,)), ((), ())),
                                 preferred_element_type=F32) * decay_ref[hd]
            inner = jnp.dot(sc.astype(BF16), vb, preferred_element_type=F32)
            st = state_ref[hd]
            cross = jnp.dot((qr * xi_ref[hd]).astype(BF16), st.astype(BF16),
                            preferred_element_type=F32)
            kz_t = (kr * zeta_ref[hd]).T.astype(BF16)
            state_ref[hd] = st * gc_ref[hd] + jnp.dot(kz_t, vb, preferred_element_type=F32)
            out = inner + cross
            ms = jnp.mean(out * out, axis=-1, keepdims=True)
            on = out * lax.rsqrt(ms + EPS) * retg_ref[:, l0:l1]
            y_ref[r0:r1, aw + l0:aw + l1] = (_silu(gr[r0:r1, l0:l1]) * on).astype(BF16)

    m = jnp.dot(y_ref[...], wout_ref[...], preferred_element_type=F32)
    o_ref[...] = x + _rms(m, g3_ref[...])


def _retention_tables(seq):
    half = HEAD_B // 2
    inv = ROPE_BASE ** (-jnp.arange(half, dtype=F32) / half)
    ang = jnp.arange(seq, dtype=F32)[:, None] * inv[None, :]
    cos = jnp.cos(ang)
    sin = jnp.sin(ang)
    cos_t = jnp.concatenate([cos, cos], axis=-1)
    sin_t = jnp.concatenate([-sin, sin], axis=-1)
    scale = HEAD_B ** -0.5
    log_gamma = jnp.log1p(-(2.0 ** (-5.0 - jnp.arange(B_HEADS, dtype=F32))))
    idx = jnp.arange(CHUNK, dtype=F32)
    diff = idx[:, None] - idx[None, :]
    decay = jnp.where(diff >= 0, jnp.exp(log_gamma[:, None, None] * jnp.maximum(diff, 0.0)), 0.0)
    xi = jnp.exp(log_gamma[:, None] * (idx + 1.0))
    zeta = jnp.exp(log_gamma[:, None] * (CHUNK - 1.0 - idx))
    gamma_c = jnp.exp(log_gamma * CHUNK)
    rep = lambda t: jnp.broadcast_to(t[:, :, None], (B_HEADS, CHUNK, LANES))
    gc = jnp.broadcast_to(gamma_c[:, None, None], (B_HEADS, 1, LANES))
    return cos_t, sin_t, decay * scale, rep(xi), rep(zeta * scale), gc


def _even_mixer(x2, bsz, seq, g2, g3, w_in, w_out, v_g, v_b, w_s, b_s, ret_g):
    n, d = x2.shape
    tm = min(TOKEN_TILE, seq)
    tiles = seq // tm
    aw = A_GROUPS * CHUNK
    bw = B_HEADS * HEAD_B
    cos_t, sin_t, decay, xi, zeta, gc = _retention_tables(seq)
    bs_rep = jnp.broadcast_to(b_s[:, :, None], (A_GROUPS, CHUNK, LANES))
    row = pl.BlockSpec((tm, d), lambda b, i: (b * tiles + i, 0))
    tab = pl.BlockSpec((tm, LANES), lambda b, i: (i, 0))
    return pl.pallas_call(
        functools.partial(_even_kernel, n_chunks=tm // CHUNK),
        grid=(bsz, tiles),
        in_specs=[row, _const_spec((1, d)), _const_spec((1, d)),
                  _const_spec(w_in.shape), _const_spec(w_out.shape),
                  _const_spec((1, aw)), _const_spec((1, aw)),
                  _const_spec((A_GROUPS, CHUNK, CHUNK)), _const_spec((A_GROUPS, CHUNK, LANES)),
                  _const_spec((1, bw)), tab, tab,
                  _const_spec((B_HEADS, CHUNK, CHUNK)), _const_spec((B_HEADS, CHUNK, LANES)),
                  _const_spec((B_HEADS, CHUNK, LANES)), _const_spec((B_HEADS, 1, LANES))],
        out_specs=row,
        out_shape=jax.ShapeDtypeStruct((n, d), F32),
        scratch_shapes=[pltpu.VMEM((B_HEADS, HEAD_B, HEAD_B), F32),
                        pltpu.VMEM((tm, aw + bw), BF16)],
        compiler_params=_params("arbitrary", "arbitrary"),
        name="even_mixer",
    )(x2, g2.reshape(1, d), g3.reshape(1, d), w_in, w_out, v_g.reshape(1, aw), v_b.reshape(1, aw),
      w_s, bs_rep, ret_g.reshape(1, bw), cos_t, sin_t, decay, xi, zeta, gc)


def _qkv_kernel(x_ref, g2_ref, w_ref, q_ref, k_ref, v_ref):
    d = x_ref.shape[1]
    h = _rms(x_ref[...], g2_ref[...]).astype(BF16)
    p = jnp.dot(h, w_ref[...], preferred_element_type=F32)
    q_ref[...] = (p[:, :d] * (HEAD_C ** -0.5)).astype(BF16)
    k_ref[...] = p[:, d:2 * d].astype(BF16)
    v_ref[...] = p[:, 2 * d:].astype(BF16)


def _qkv(x2, g2, w_qkv):
    n, d = x2.shape
    tm = min(TOKEN_TILE, n)
    row = pl.BlockSpec((tm, d), lambda i: (i, 0))
    out = jax.ShapeDtypeStruct((n, d), BF16)
    return pl.pallas_call(
        _qkv_kernel,
        grid=(n // tm,),
        in_specs=[row, _const_spec((1, d)), _const_spec(w_qkv.shape)],
        out_specs=[row, row, row],
        out_shape=[out, out, out],
        compiler_params=_params("arbitrary"),
        name="qkv_proj",
    )(x2, g2.reshape(1, d), w_qkv)


SKIP_LOG = 88.0
MASKED_LOG = -1e30
SUB = 64
WIN = 2 * CHUNK


def _softplus(z):
    return jnp.maximum(z, 0.0) + jnp.log(1.0 + jnp.exp(-jnp.abs(z)))


def _hi_lo(x):
    hi = x.astype(BF16)
    lo = (x - hi.astype(F32)).astype(BF16)
    return jnp.concatenate([hi, lo], axis=1)


def _sb_kernel(q_ref, k_ref, v_ref, rhs_ref, o_ref, ls_scr, hl_scr, a_scr, *, group):
    seq = q_ref.shape[1]
    nsub = seq // SUB
    reach = WIN - SUB
    pair_rows = 2 * SUB
    key_blocks = WIN // CHUNK
    first = lax.broadcasted_iota(jnp.int32, (SUB, LANES), 1) < HEAD_C
    col_minus_row = (lax.broadcasted_iota(jnp.int32, (SUB, WIN), 1)
                     - lax.broadcasted_iota(jnp.int32, (SUB, WIN), 0))
    tail_col = lax.broadcasted_iota(jnp.int32, (pair_rows, CHUNK), 1)
    contract_last = (((1,), (1,)), ((), ()))

    def at(row):
        return row if isinstance(row, int) else pl.multiple_of(row, SUB)

    def load_q(u):
        qp = q_ref[0, pl.ds(at(u * SUB), SUB), :]
        zero = jnp.zeros_like(qp)
        return jnp.concatenate([jnp.where(first, qp, zero), jnp.where(first, zero, qp)], axis=0)

    def windows(subs, starts, offs):
        for g, (u, w0) in enumerate(zip(subs, starts)):
            kwin = k_ref[0, pl.ds(w0, WIN), :]
            ls_scr[g * pair_rows:(g + 1) * pair_rows, :] = lax.dot_general(
                load_q(u), kwin, contract_last, preferred_element_type=F32)
        for g, off in enumerate(offs):
            sl = slice(g * pair_rows, (g + 1) * pair_rows)
            for b in range(key_blocks):
                cols = slice(b * CHUNK, (b + 1) * CHUNK)
                z = ls_scr[sl, cols].reshape(2, SUB, CHUNK)
                sp = _softplus(z)
                ls = z - sp
                if (b + 1) * CHUNK - 1 >= off:
                    valid = (col_minus_row[:, cols] < off)[None]
                    sp = jnp.where(valid, sp, 0.0)
                    ls = jnp.where(valid, ls, MASKED_LOG)
                ls_scr[sl, cols] = ls.reshape(pair_rows, CHUNK)
                hl_scr[b, sl, :] = _hi_lo(sp.reshape(pair_rows, CHUNK))
        r = [jnp.dot(hl_scr[b], rhs_ref[...], preferred_element_type=F32) for b in range(key_blocks)]
        acc = None
        for b in reversed(range(key_blocks)):
            later = r[b][:, :CHUNK] if acc is None else r[b][:, :CHUNK] + acc
            cols = slice(b * CHUNK, (b + 1) * CHUNK)
            a_scr[:, cols] = jnp.exp(ls_scr[:, cols] - later).astype(BF16)
            acc = r[b][:, CHUNK:] if acc is None else acc + r[b][:, CHUNK:]
        out = []
        for g, w0 in enumerate(starts):
            sl = slice(g * pair_rows, (g + 1) * pair_rows)
            vwin = v_ref[0, pl.ds(w0, WIN), :]
            out.append((acc[sl, :], jnp.dot(a_scr[sl, :], vwin, preferred_element_type=F32)))
        return out

    def tail(u, end, acc, o):
        q2 = load_q(u)

        def cond(c):
            return jnp.logical_and(c[0] > 0, c[1] < SKIP_LOG)

        def body(c):
            end, _, acc, o = c
            s0 = jnp.maximum(end - CHUNK, 0)
            kb = k_ref[0, pl.ds(at(s0), CHUNK), :]
            vb = v_ref[0, pl.ds(at(s0), CHUNK), :]
            fresh = tail_col < end - s0
            z = lax.dot_general(q2, kb, contract_last, preferred_element_type=F32)
            sp = _softplus(z)
            r = jnp.dot(_hi_lo(jnp.where(fresh, sp, 0.0)), rhs_ref[...], preferred_element_type=F32)
            a = jnp.where(fresh, jnp.exp(z - sp - r[:, :CHUNK] - acc), 0.0)
            acc = acc + r[:, CHUNK:]
            return s0, jnp.min(acc), acc, o + jnp.dot(a.astype(BF16), vb, preferred_element_type=F32)

        return lax.while_loop(cond, body, (end, jnp.min(acc), acc, o))[3]

    def store(u, o):
        o_ref[0, pl.ds(at(u * SUB), SUB), :] = jnp.where(first, o[:SUB], o[SUB:]).astype(BF16)

    def do_group(subs, starts, offs):
        states = windows(subs, starts, offs)
        for u, (_, o) in zip(subs, states):
            store(u, o)
        open_ = [(u, w0, acc, o) for u, w0, (acc, o) in zip(subs, starts, states)
                 if not (isinstance(w0, int) and w0 == 0)]
        if open_:
            lowest = functools.reduce(jnp.minimum, [acc for _, _, acc, _ in open_])

            @pl.when(jnp.min(lowest) < SKIP_LOG)
            def _():
                for u, w0, acc, o in open_:
                    store(u, tail(u, w0, acc, o))

    lead = list(range(group))
    lead_start = [max(u * SUB - reach, 0) for u in lead]
    do_group(lead, lead_start, [u * SUB - s for u, s in zip(lead, lead_start)])

    def loop_body(n, carry):
        subs = [n * group + t for t in range(group)]
        do_group(subs, [at(u * SUB - reach) for u in subs], [reach] * group)
        return carry

    lax.fori_loop(1, nsub // group, loop_body, 0)


def _stick_breaking(q, k, v, bsz, seq):
    d = q.shape[-1]
    pairs = d // LANES
    group = 8
    assert (seq // SUB) % group == 0 and group * SUB >= WIN and seq >= WIN
    q3, k3, v3 = (t.reshape(bsz, seq, d) for t in (q, k, v))
    idx = jnp.arange(CHUNK)
    later = (idx[:, None] > idx[None, :]).astype(BF16)
    rhs = jnp.concatenate([later, jnp.ones((CHUNK, LANES), BF16)], axis=1)
    rhs2 = jnp.concatenate([rhs, rhs], axis=0)
    grows = group * 2 * SUB
    spec = pl.BlockSpec((1, seq, LANES), lambda b, p: (b, 0, p))
    out = pl.pallas_call(
        functools.partial(_sb_kernel, group=group),
        grid=(bsz, pairs),
        in_specs=[spec, spec, spec, _const_spec(rhs2.shape)],
        out_specs=spec,
        out_shape=jax.ShapeDtypeStruct((bsz, seq, d), BF16),
        scratch_shapes=[pltpu.VMEM((grows, WIN), F32),
                        pltpu.VMEM((WIN // CHUNK, grows, 2 * CHUNK), BF16),
                        pltpu.VMEM((grows, WIN), BF16)],
        compiler_params=_params("arbitrary", "arbitrary"),
        name="stick_breaking",
    )(q3, k3, v3, rhs2)
    return out.reshape(bsz * seq, d)


def _outproj_kernel(x_ref, a_ref, w_ref, g_ref, o_ref):
    m = jnp.dot(a_ref[...], w_ref[...], preferred_element_type=F32)
    o_ref[...] = x_ref[...] + _rms(m, g_ref[...])


def _outproj(x2, a2, w_out, g3):
    n, d = x2.shape
    tm = min(TOKEN_TILE, n)
    row = pl.BlockSpec((tm, d), lambda i: (i, 0))
    return pl.pallas_call(
        _outproj_kernel,
        grid=(n // tm,),
        in_specs=[row, row, _const_spec(w_out.shape), _const_spec((1, d))],
        out_specs=row,
        out_shape=jax.ShapeDtypeStruct((n, d), F32),
        compiler_params=_params("arbitrary"),
        name="out_proj",
    )(x2, a2, w_out, g3.reshape(1, d))


def kernel(x, norm_g, ffn_w_gate, ffn_w_up, ffn_w_down, ab_w_in, ab_w_out, gmlp_v_norm_g,
           gmlp_v_norm_b, gmlp_w_s, gmlp_b_s, ret_norm_g, sb_w_qkv, sb_w_out):
    bsz, seq, d = x.shape
    depth = norm_g.shape[0]
    x2 = x.reshape(bsz * seq, d)
    wg, wu, wd = (w.astype(BF16) for w in (ffn_w_gate, ffn_w_up, ffn_w_down))
    for layer in range(depth):
        g = norm_g[layer]
        x2 = _ffn(x2, g[0], g[1], wg[layer, 0], wu[layer, 0], wd[layer, 0])
        if layer % 2 == 0:
            e = layer // 2
            x2 = _even_mixer(x2, bsz, seq, g[2], g[3], ab_w_in[e].astype(BF16),
                             ab_w_out[e].astype(BF16), gmlp_v_norm_g[e], gmlp_v_norm_b[e],
                             gmlp_w_s[e], gmlp_b_s[e], ret_norm_g[e])
        else:
            o = layer // 2
            q, k, v = _qkv(x2, g[2], sb_w_qkv[o].astype(BF16))
            att = _stick_breaking(q, k, v, bsz, seq)
            x2 = _outproj(x2, att, sb_w_out[o].astype(BF16), g[3])
        x2 = _ffn(x2, g[4], g[5], wg[layer, 1], wu[layer, 1], wd[layer, 1])
    return x2.reshape(bsz, seq, d)
```

```python
import functools
import math

import jax
import jax.numpy as jnp
from jax import lax
from jax.experimental import pallas as pl
from jax.experimental.pallas import tpu as pltpu

F32 = jnp.float32
BF16 = jnp.bfloat16

EPS = 1e-6
CHUNK = 128
A_GROUPS = 4
B_HEADS = 4
HEAD_B = 128
C_HEADS = 16
HEAD_C = 64
ROPE_BASE = 10000.0
LANES = 128
VMEM_LIMIT = 56 * 1024 * 1024

TOKEN_TILE = 512


def _rms(x, g):
    ms = jnp.mean(x * x, axis=-1, keepdims=True)
    return x * lax.rsqrt(ms + EPS) * g


def _silu(x):
    return x / (1.0 + jnp.exp(-x))


def _gelu_tanh(x):
    c = 0.7978845608028654
    return 0.5 * x * (1.0 + jnp.tanh(c * (x + 0.044715 * (x * x * x))))


def _const_spec(shape):
    nd = len(shape)
    return pl.BlockSpec(shape, lambda *_: (0,) * nd, pipeline_mode=pl.Buffered(1))


def _params(*sem):
    return pltpu.CompilerParams(dimension_semantics=sem, vmem_limit_bytes=VMEM_LIMIT)


def _ffn_kernel(*refs, mix_in, qkv_out):
    refs = list(refs)
    x_ref = refs.pop(0)
    att_ref, wo_ref, gmix_ref = (refs.pop(0), refs.pop(0), refs.pop(0)) if mix_in else (None,) * 3
    gpre_ref, gpost_ref, wg_ref, wu_ref, wd_ref = (refs.pop(0) for _ in range(5))
    gqkv_ref, wqkv_ref = (refs.pop(0), refs.pop(0)) if qkv_out else (None, None)
    o_ref = refs.pop(0)

    x = x_ref[...]
    if mix_in:
        x = x + _rms(jnp.dot(att_ref[...], wo_ref[...], preferred_element_type=F32), gmix_ref[...])
    xn = _rms(x, gpre_ref[...]).astype(BF16)
    g = jnp.dot(xn, wg_ref[...], preferred_element_type=F32)
    u = jnp.dot(xn, wu_ref[...], preferred_element_type=F32)
    h = (_silu(g) * u).astype(BF16)
    f = jnp.dot(h, wd_ref[...], preferred_element_type=F32)
    x = x + 0.5 * _rms(f, gpost_ref[...])
    o_ref[...] = x
    if qkv_out:
        q_ref, k_ref, v_ref = refs
        d = x.shape[1]
        p = jnp.dot(_rms(x, gqkv_ref[...]).astype(BF16), wqkv_ref[...], preferred_element_type=F32)
        q_ref[...] = (p[:, :d] * (HEAD_C ** -0.5)).astype(BF16)
        k_ref[...] = p[:, d:2 * d].astype(BF16)
        v_ref[...] = p[:, 2 * d:].astype(BF16)


def _ffn(x2, g_pre, g_post, wg, wu, wd, mix=None, qkv=None):
    n, d = x2.shape
    dff = wg.shape[1]
    tm = min(TOKEN_TILE, n)
    row = pl.BlockSpec((tm, d), lambda i: (i, 0))
    vec = _const_spec((1, d))
    args, specs = [x2], [row]
    if mix is not None:
        att, w_o, g_mix = mix
        args += [att, w_o, g_mix.reshape(1, d)]
        specs += [row, _const_spec(w_o.shape), vec]
    args += [g_pre.reshape(1, d), g_post.reshape(1, d), wg, wu, wd]
    specs += [vec, vec, _const_spec((d, dff)), _const_spec((d, dff)), _const_spec((dff, d))]
    out_specs, out_shape = [row], [jax.ShapeDtypeStruct((n, d), F32)]
    if qkv is not None:
        g_qkv, w_qkv = qkv
        args += [g_qkv.reshape(1, d), w_qkv]
        specs += [vec, _const_spec(w_qkv.shape)]
        out_specs += [row] * 3
        out_shape += [jax.ShapeDtypeStruct((n, d), BF16)] * 3
    out = pl.pallas_call(
        functools.partial(_ffn_kernel, mix_in=mix is not None, qkv_out=qkv is not None),
        grid=(n // tm,),
        in_specs=specs,
        out_specs=out_specs,
        out_shape=out_shape,
        compiler_params=_params("arbitrary"),
        name="ffn",
    )(*args)
    return out if qkv is not None else out[0]


def _even_kernel(x_ref, g2_ref, g3_ref, win_ref, wout_ref, vg_ref, vb_ref, ws_ref, bs_ref,
                 retg_ref, cos_ref, sin_ref, decay_ref, xi_ref, zeta_ref, gc_ref,
                 o_ref, state_ref, y_ref, *, n_chunks):
    @pl.when(pl.program_id(1) == 0)
    def _():
        state_ref[...] = jnp.zeros_like(state_ref)

    aw = A_GROUPS * CHUNK
    bw = B_HEADS * HEAD_B
    x = x_ref[...]
    h = _rms(x, g2_ref[...]).astype(BF16)
    p = jnp.dot(h, win_ref[...], preferred_element_type=F32)
    a = _gelu_tanh(p[:, :2 * aw])
    u = a[:, :aw]
    v = a[:, aw:]
    mu = jnp.mean(v, axis=-1, keepdims=True)
    vc = v - mu
    var = jnp.mean(vc * vc, axis=-1, keepdims=True)
    vn = vc * lax.rsqrt(var + EPS) * vg_ref[...] + vb_ref[...]
    off = 2 * aw
    q = p[:, off:off + bw]
    k = p[:, off + bw:off + 2 * bw]
    vr = p[:, off + 2 * bw:off + 3 * bw]
    gr = p[:, off + 3 * bw:off + 4 * bw]

    row = lax.broadcasted_iota(jnp.int32, (CHUNK, CHUNK), 0)
    col = lax.broadcasted_iota(jnp.int32, (CHUNK, CHUNK), 1)
    causal = row >= col

    for c in range(n_chunks):
        r0, r1 = c * CHUNK, (c + 1) * CHUNK
        for g in range(A_GROUPS):
            l0, l1 = g * CHUNK, (g + 1) * CHUNK
            w = jnp.where(causal, ws_ref[g], 0.0).astype(BF16)
            s = jnp.dot(w, vn[r0:r1, l0:l1].astype(BF16), preferred_element_type=F32) + bs_ref[g]
            y_ref[r0:r1, l0:l1] = (u[r0:r1, l0:l1] * s).astype(BF16)
        cosc = cos_ref[r0:r1, :]
        sinc = sin_ref[r0:r1, :]
        for hd in range(B_HEADS):
            l0, l1 = hd * HEAD_B, (hd + 1) * HEAD_B
            qc = q[r0:r1, l0:l1]
            kc = k[r0:r1, l0:l1]
            qr = qc * cosc + pltpu.roll(qc, HEAD_B // 2, axis=1) * sinc
            kr = kc * cosc + pltpu.roll(kc, HEAD_B // 2, axis=1) * sinc
            vb = vr[r0:r1, l0:l1].astype(BF16)
            sc = lax.dot_general(qr.astype(BF16), kr.astype(BF16), (((1,), (1,)), ((), ())),
                                 preferred_element_type=F32) * decay_ref[hd]
            inner = jnp.dot(sc.astype(BF16), vb, preferred_element_type=F32)
            st = state_ref[hd]
            cross = jnp.dot((qr * xi_ref[hd]).astype(BF16), st.astype(BF16),
                            preferred_element_type=F32)
            kz_t = (kr * zeta_ref[hd]).T.astype(BF16)
            state_ref[hd] = st * gc_ref[hd] + jnp.dot(kz_t, vb, preferred_element_type=F32)
            out = inner + cross
            ms = jnp.mean(out * out, axis=-1, keepdims=True)
            on = out * lax.rsqrt(ms + EPS) * retg_ref[:, l0:l1]
            y_ref[r0:r1, aw + l0:aw + l1] = (_silu(gr[r0:r1, l0:l1]) * on).astype(BF16)

    m = jnp.dot(y_ref[...], wout_ref[...], preferred_element_type=F32)
    o_ref[...] = x + _rms(m, g3_ref[...])


def _retention_tables(seq):
    half = HEAD_B // 2
    inv = ROPE_BASE ** (-jnp.arange(half, dtype=F32) / half)
    ang = jnp.arange(seq, dtype=F32)[:, None] * inv[None, :]
    cos = jnp.cos(ang)
    sin = jnp.sin(ang)
    cos_t = jnp.concatenate([cos, cos], axis=-1)
    sin_t = jnp.concatenate([-sin, sin], axis=-1)
    scale = HEAD_B ** -0.5
    log_gamma = jnp.log1p(-(2.0 ** (-5.0 - jnp.arange(B_HEADS, dtype=F32))))
    idx = jnp.arange(CHUNK, dtype=F32)
    diff = idx[:, None] - idx[None, :]
    decay = jnp.where(diff >= 0, jnp.exp(log_gamma[:, None, None] * jnp.maximum(diff, 0.0)), 0.0)
    xi = jnp.exp(log_gamma[:, None] * (idx + 1.0))
    zeta = jnp.exp(log_gamma[:, None] * (CHUNK - 1.0 - idx))
    gamma_c = jnp.exp(log_gamma * CHUNK)
    rep = lambda t: jnp.broadcast_to(t[:, :, None], (B_HEADS, CHUNK, LANES))
    gc = jnp.broadcast_to(gamma_c[:, None, None], (B_HEADS, 1, LANES))
    return cos_t, sin_t, decay * scale, rep(xi), rep(zeta * scale), gc


def _even_mixer(x2, bsz, seq, g2, g3, w_in, w_out, v_g, v_b, w_s, b_s, ret_g):
    n, d = x2.shape
    tm = min(TOKEN_TILE, seq)
    tiles = seq // tm
    aw = A_GROUPS * CHUNK
    bw = B_HEADS * HEAD_B
    cos_t, sin_t, decay, xi, zeta, gc = _retention_tables(seq)
    bs_rep = jnp.broadcast_to(b_s[:, :, None], (A_GROUPS, CHUNK, LANES))
    row = pl.BlockSpec((tm, d), lambda b, i: (b * tiles + i, 0))
    tab = pl.BlockSpec((tm, LANES), lambda b, i: (i, 0))
    return pl.pallas_call(
        functools.partial(_even_kernel, n_chunks=tm // CHUNK),
        grid=(bsz, tiles),
        in_specs=[row, _const_spec((1, d)), _const_spec((1, d)),
                  _const_spec(w_in.shape), _const_spec(w_out.shape),
                  _const_spec((1, aw)), _const_spec((1, aw)),
                  _const_spec((A_GROUPS, CHUNK, CHUNK)), _const_spec((A_GROUPS, CHUNK, LANES)),
                  _const_spec((1, bw)), tab, tab,
                  _const_spec((B_HEADS, CHUNK, CHUNK)), _const_spec((B_HEADS, CHUNK, LANES)),
                  _const_spec((B_HEADS, CHUNK, LANES)), _const_spec((B_HEADS, 1, LANES))],
        out_specs=row,
        out_shape=jax.ShapeDtypeStruct((n, d), F32),
        scratch_shapes=[pltpu.VMEM((B_HEADS, HEAD_B, HEAD_B), F32),
                        pltpu.VMEM((tm, aw + bw), BF16)],
        compiler_params=_params("arbitrary", "arbitrary"),
        name="even_mixer",
    )(x2, g2.reshape(1, d), g3.reshape(1, d), w_in, w_out, v_g.reshape(1, aw), v_b.reshape(1, aw),
      w_s, bs_rep, ret_g.reshape(1, bw), cos_t, sin_t, decay, xi, zeta, gc)


SKIP_LOG = 88.0
MASKED_LOG = -1e30
SUB = 64
WIN = 2 * CHUNK


def _softplus(z):
    sign_bit = jnp.uint32(0x80000000)
    neg_abs = lax.bitcast_convert_type(lax.bitcast_convert_type(z, jnp.uint32) | sign_bit, F32)
    return jnp.maximum(z, 0.0) + jnp.log(1.0 + jnp.exp(neg_abs))


def _hi_lo(x):
    hi = x.astype(BF16)
    lo = (x - hi.astype(F32)).astype(BF16)
    return jnp.concatenate([hi, lo], axis=1)


def _sb_kernel(q_ref, k_ref, v_ref, rhs_ref, o_ref, ls_scr, hl_scr, a_scr, *, group):
    seq = q_ref.shape[1]
    nsub = seq // SUB
    reach = WIN - SUB
    pair_rows = 2 * SUB
    key_blocks = WIN // CHUNK
    first = lax.broadcasted_iota(jnp.int32, (SUB, LANES), 1) < HEAD_C
    col_minus_row = (lax.broadcasted_iota(jnp.int32, (SUB, WIN), 1)
                     - lax.broadcasted_iota(jnp.int32, (SUB, WIN), 0))
    tail_col = lax.broadcasted_iota(jnp.int32, (pair_rows, CHUNK), 1)
    contract_last = (((1,), (1,)), ((), ()))

    def at(row):
        return row if isinstance(row, int) else pl.multiple_of(row, SUB)

    def load_q(u):
        qp = q_ref[0, pl.ds(at(u * SUB), SUB), :]
        zero = jnp.zeros_like(qp)
        return jnp.concatenate([jnp.where(first, qp, zero), jnp.where(first, zero, qp)], axis=0)

    def windows(subs, starts, offs):
        for g, (u, w0) in enumerate(zip(subs, starts)):
            kwin = k_ref[0, pl.ds(w0, WIN), :]
            ls_scr[g * pair_rows:(g + 1) * pair_rows, :] = lax.dot_general(
                load_q(u), kwin, contract_last, preferred_element_type=F32)
        for g, off in enumerate(offs):
            sl = slice(g * pair_rows, (g + 1) * pair_rows)
            for b in range(key_blocks):
                cols = slice(b * CHUNK, (b + 1) * CHUNK)
                z = ls_scr[sl, cols].reshape(2, SUB, CHUNK)
                sp = _softplus(z)
                ls = z - sp
                if (b + 1) * CHUNK - 1 >= off:
                    valid = (col_minus_row[:, cols] < off)[None]
                    sp = jnp.where(valid, sp, 0.0)
                    ls = jnp.where(valid, ls, MASKED_LOG)
                ls_scr[sl, cols] = ls.reshape(pair_rows, CHUNK)
                hl_scr[b, sl, :] = _hi_lo(sp.reshape(pair_rows, CHUNK))
        r = [jnp.dot(hl_scr[b], rhs_ref[...], preferred_element_type=F32) for b in range(key_blocks)]
        acc = None
        for b in reversed(range(key_blocks)):
            later = r[b][:, :CHUNK] if acc is None else r[b][:, :CHUNK] + acc
            cols = slice(b * CHUNK, (b + 1) * CHUNK)
            a_scr[:, cols] = jnp.exp(ls_scr[:, cols] - later).astype(BF16)
            acc = r[b][:, CHUNK:] if acc is None else acc + r[b][:, CHUNK:]
        out = []
        for g, w0 in enumerate(starts):
            sl = slice(g * pair_rows, (g + 1) * pair_rows)
            vwin = v_ref[0, pl.ds(w0, WIN), :]
            out.append((acc[sl, :], jnp.dot(a_scr[sl, :], vwin, preferred_element_type=F32)))
        return out

    def tail(u, end, acc, o):
        q2 = load_q(u)

        def cond(c):
            return jnp.logical_and(c[0] > 0, c[1] < SKIP_LOG)

        def body(c):
            end, _, acc, o = c
            s0 = jnp.maximum(end - CHUNK, 0)
            kb = k_ref[0, pl.ds(at(s0), CHUNK), :]
            vb = v_ref[0, pl.ds(at(s0), CHUNK), :]
            fresh = tail_col < end - s0
            z = lax.dot_general(q2, kb, contract_last, preferred_element_type=F32)
            sp = _softplus(z)
            r = jnp.dot(_hi_lo(jnp.where(fresh, sp, 0.0)), rhs_ref[...], preferred_element_type=F32)
            a = jnp.where(fresh, jnp.exp(z - sp - r[:, :CHUNK] - acc), 0.0)
            acc = acc + r[:, CHUNK:]
            return s0, jnp.min(acc), acc, o + jnp.dot(a.astype(BF16), vb, preferred_element_type=F32)

        return lax.while_loop(cond, body, (end, jnp.min(acc), acc, o))[3]

    def store(u, o):
        o_ref[0, pl.ds(at(u * SUB), SUB), :] = jnp.where(first, o[:SUB], o[SUB:]).astype(BF16)

    def do_group(subs, starts, offs):
        states = windows(subs, starts, offs)
        for u, (_, o) in zip(subs, states):
            store(u, o)
        open_ = [(u, w0, acc, o) for u, w0, (acc, o) in zip(subs, starts, states)
                 if not (isinstance(w0, int) and w0 == 0)]
        if open_:
            lowest = functools.reduce(jnp.minimum, [acc for _, _, acc, _ in open_])

            @pl.when(jnp.min(lowest) < SKIP_LOG)
            def _():
                for u, w0, acc, o in open_:
                    store(u, tail(u, w0, acc, o))

    lead = list(range(group))
    lead_start = [max(u * SUB - reach, 0) for u in lead]
    do_group(lead, lead_start, [u * SUB - s for u, s in zip(lead, lead_start)])

    def loop_body(n, carry):
        subs = [n * group + t for t in range(group)]
        do_group(subs, [at(u * SUB - reach) for u in subs], [reach] * group)
        return carry

    lax.fori_loop(1, nsub // group, loop_body, 0)


def _stick_breaking(q, k, v, bsz, seq):
    d = q.shape[-1]
    pairs = d // LANES
    group = math.gcd(16, seq // SUB)
    assert group * SUB >= WIN
    q3, k3, v3 = (t.reshape(bsz, seq, d) for t in (q, k, v))
    idx = jnp.arange(CHUNK)
    later = (idx[:, None] > idx[None, :]).astype(BF16)
    rhs = jnp.concatenate([later, jnp.ones((CHUNK, LANES), BF16)], axis=1)
    rhs2 = jnp.concatenate([rhs, rhs], axis=0)
    grows = group * 2 * SUB
    spec = pl.BlockSpec((1, seq, LANES), lambda b, p: (b, 0, p))
    out = pl.pallas_call(
        functools.partial(_sb_kernel, group=group),
        grid=(bsz, pairs),
        in_specs=[spec, spec, spec, _const_spec(rhs2.shape)],
        out_specs=spec,
        out_shape=jax.ShapeDtypeStruct((bsz, seq, d), BF16),
        scratch_shapes=[pltpu.VMEM((grows, WIN), F32),
                        pltpu.VMEM((WIN // CHUNK, grows, 2 * CHUNK), BF16),
                        pltpu.VMEM((grows, WIN), BF16)],
        compiler_params=_params("arbitrary", "arbitrary"),
        name="stick_breaking",
    )(q3, k3, v3, rhs2)
    return out.reshape(bsz * seq, d)


def kernel(x, norm_g, ffn_w_gate, ffn_w_up, ffn_w_down, ab_w_in, ab_w_out, gmlp_v_norm_g,
           gmlp_v_norm_b, gmlp_w_s, gmlp_b_s, ret_norm_g, sb_w_qkv, sb_w_out):
    bsz, seq, d = x.shape
    depth = norm_g.shape[0]
    x2 = x.reshape(bsz * seq, d)
    wg, wu, wd = (w.astype(BF16) for w in (ffn_w_gate, ffn_w_up, ffn_w_down))
    for layer in range(depth):
        g = norm_g[layer]
        pre = (g[0], g[1], wg[layer, 0], wu[layer, 0], wd[layer, 0])
        post = (g[4], g[5], wg[layer, 1], wu[layer, 1], wd[layer, 1])
        if layer % 2 == 0:
            e = layer // 2
            x2 = _ffn(x2, *pre)
            x2 = _even_mixer(x2, bsz, seq, g[2], g[3], ab_w_in[e].astype(BF16),
                             ab_w_out[e].astype(BF16), gmlp_v_norm_g[e], gmlp_v_norm_b[e],
                             gmlp_w_s[e], gmlp_b_s[e], ret_norm_g[e])
            x2 = _ffn(x2, *post)
        else:
            o = layer // 2
            x2, q, k, v = _ffn(x2, *pre, qkv=(g[2], sb_w_qkv[o].astype(BF16)))
            att = _stick_breaking(q, k, v, bsz, seq)
            x2 = _ffn(x2, *post, mix=(att, sb_w_out[o].astype(BF16), g[3]))
    return x2.reshape(bsz, seq, d)
```

```python
import functools
import math

import jax
import jax.numpy as jnp
import numpy as np
from jax import lax
from jax.experimental import pallas as pl
from jax.experimental.pallas import tpu as pltpu

F32 = jnp.float32
BF16 = jnp.bfloat16

EPS = 1e-6
CHUNK = 128
A_GROUPS = 4
B_HEADS = 4
HEAD_B = 128
C_HEADS = 16
HEAD_C = 64
ROPE_BASE = 10000.0
LANES = 128
VMEM_LIMIT = 56 * 1024 * 1024

TOKEN_TILE = 512


def _rms(x, g):
    ms = jnp.mean(x * x, axis=-1, keepdims=True)
    return x * lax.rsqrt(ms + EPS) * g


def _silu(x):
    return x / (1.0 + jnp.exp(-x))


def _gelu_tanh(x):
    c = 0.7978845608028654
    return 0.5 * x * (1.0 + jnp.tanh(c * (x + 0.044715 * (x * x * x))))


def _const_spec(shape, lead=()):
    block = (None,) * len(lead) + tuple(shape[len(lead):])
    index = tuple(lead) + (0,) * (len(shape) - len(lead))
    return pl.BlockSpec(block, lambda *_: index, pipeline_mode=pl.Buffered(1))


def _params(*sem):
    return pltpu.CompilerParams(dimension_semantics=sem, vmem_limit_bytes=VMEM_LIMIT)


def _ffn_kernel(*refs, mix_in, qkv_out):
    refs = list(refs)
    x_ref = refs.pop(0)
    att_ref, wo_ref, gmix_ref = (refs.pop(0), refs.pop(0), refs.pop(0)) if mix_in else (None,) * 3
    gpre_ref, gpost_ref, wg_ref, wu_ref, wd_ref = (refs.pop(0) for _ in range(5))
    gqkv_ref, wqkv_ref = (refs.pop(0), refs.pop(0)) if qkv_out else (None, None)
    o_ref = refs.pop(0)

    x = x_ref[...]
    if mix_in:
        x = x + _rms(jnp.dot(att_ref[...], wo_ref[...], preferred_element_type=F32), gmix_ref[...])
    xn = _rms(x, gpre_ref[...]).astype(BF16)
    g = jnp.dot(xn, wg_ref[...], preferred_element_type=F32)
    u = jnp.dot(xn, wu_ref[...], preferred_element_type=F32)
    h = (_silu(g) * u).astype(BF16)
    f = jnp.dot(h, wd_ref[...], preferred_element_type=F32)
    x = x + 0.5 * _rms(f, gpost_ref[...])
    o_ref[...] = x
    if qkv_out:
        q_ref, k_ref, v_ref = refs
        d = x.shape[1]
        p = jnp.dot(_rms(x, gqkv_ref[...]).astype(BF16), wqkv_ref[...], preferred_element_type=F32)
        q_ref[...] = (p[:, :d] * (HEAD_C ** -0.5)).astype(BF16)
        k_ref[...] = p[:, d:2 * d].astype(BF16)
        v_ref[...] = p[:, 2 * d:].astype(BF16)


def _ffn(x2, g_pre, g_post, weights, which, mix=None, qkv=None):
    n, d = x2.shape
    wg, wu, wd = weights
    dff = wg.shape[-1]
    tm = min(TOKEN_TILE, n)
    row = pl.BlockSpec((tm, d), lambda i: (i, 0))
    vec = _const_spec((1, d))
    args, specs = [x2], [row]
    if mix is not None:
        att, w_o, g_mix = mix
        args += [att, w_o, g_mix.reshape(1, d)]
        specs += [row, _const_spec(w_o.shape), vec]
    args += [g_pre.reshape(1, d), g_post.reshape(1, d), wg, wu, wd]
    specs += [vec, vec] + [_const_spec(w.shape, which) for w in weights]
    out_specs, out_shape = [row], [jax.ShapeDtypeStruct((n, d), F32)]
    if qkv is not None:
        g_qkv, w_qkv = qkv
        args += [g_qkv.reshape(1, d), w_qkv]
        specs += [vec, _const_spec(w_qkv.shape)]
        out_specs += [row] * 3
        out_shape += [jax.ShapeDtypeStruct((n, d), BF16)] * 3
    out = pl.pallas_call(
        functools.partial(_ffn_kernel, mix_in=mix is not None, qkv_out=qkv is not None),
        grid=(n // tm,),
        in_specs=specs,
        out_specs=out_specs,
        out_shape=out_shape,
        compiler_params=_params("arbitrary"),
        name="ffn",
    )(*args)
    return out if qkv is not None else out[0]


def _even_kernel(x_ref, g2_ref, g3_ref, win_ref, wout_ref, vg_ref, vb_ref, ws_ref, bs_ref,
                 retg_ref, cos_ref, sin_ref, decay_ref, xi_ref, zeta_ref, gc_ref,
                 o_ref, state_ref, y_ref, *, n_chunks):
    @pl.when(pl.program_id(1) == 0)
    def _():
        state_ref[...] = jnp.zeros_like(state_ref)

    aw = A_GROUPS * CHUNK
    bw = B_HEADS * HEAD_B
    x = x_ref[...]
    h = _rms(x, g2_ref[...]).astype(BF16)
    p = jnp.dot(h, win_ref[...], preferred_element_type=F32)
    a = _gelu_tanh(p[:, :2 * aw])
    u = a[:, :aw]
    v = a[:, aw:]
    mu = jnp.mean(v, axis=-1, keepdims=True)
    vc = v - mu
    var = jnp.mean(vc * vc, axis=-1, keepdims=True)
    vn = vc * lax.rsqrt(var + EPS) * vg_ref[...] + vb_ref[...]
    off = 2 * aw
    q = p[:, off:off + bw]
    k = p[:, off + bw:off + 2 * bw]
    vr = p[:, off + 2 * bw:off + 3 * bw]
    gr = p[:, off + 3 * bw:off + 4 * bw]

    row = lax.broadcasted_iota(jnp.int32, (CHUNK, CHUNK), 0)
    col = lax.broadcasted_iota(jnp.int32, (CHUNK, CHUNK), 1)
    causal = row >= col

    for c in range(n_chunks):
        r0, r1 = c * CHUNK, (c + 1) * CHUNK
        for g in range(A_GROUPS):
            l0, l1 = g * CHUNK, (g + 1) * CHUNK
            w = jnp.where(causal, ws_ref[g], 0.0).astype(BF16)
            s = jnp.dot(w, vn[r0:r1, l0:l1].astype(BF16), preferred_element_type=F32) + bs_ref[g]
            y_ref[r0:r1, l0:l1] = (u[r0:r1, l0:l1] * s).astype(BF16)
        cosc = cos_ref[r0:r1, :]
        sinc = sin_ref[r0:r1, :]
        for hd in range(B_HEADS):
            l0, l1 = hd * HEAD_B, (hd + 1) * HEAD_B
            qc = q[r0:r1, l0:l1]
            kc = k[r0:r1, l0:l1]
            qr = qc * cosc + pltpu.roll(qc, HEAD_B // 2, axis=1) * sinc
            kr = kc * cosc + pltpu.roll(kc, HEAD_B // 2, axis=1) * sinc
            vb = vr[r0:r1, l0:l1].astype(BF16)
            sc = lax.dot_general(qr.astype(BF16), kr.astype(BF16), (((1,), (1,)), ((), ())),
                                 preferred_element_type=F32) * decay_ref[hd]
            inner = jnp.dot(sc.astype(BF16), vb, preferred_element_type=F32)
            st = state_ref[hd]
            cross = jnp.dot((qr * xi_ref[hd]).astype(BF16), st.astype(BF16),
                            preferred_element_type=F32)
            kz_t = (kr * zeta_ref[hd]).T.astype(BF16)
            state_ref[hd] = st * gc_ref[hd] + jnp.dot(kz_t, vb, preferred_element_type=F32)
            out = inner + cross
            ms = jnp.mean(out * out, axis=-1, keepdims=True)
            on = out * lax.rsqrt(ms + EPS) * retg_ref[:, l0:l1]
            y_ref[r0:r1, aw + l0:aw + l1] = (_silu(gr[r0:r1, l0:l1]) * on).astype(BF16)

    m = jnp.dot(y_ref[...], wout_ref[...], preferred_element_type=F32)
    o_ref[...] = x + _rms(m, g3_ref[...])


def _retention_tables(seq):
    half = HEAD_B // 2
    inv = ROPE_BASE ** (-jnp.arange(half, dtype=F32) / half)
    ang = jnp.arange(seq, dtype=F32)[:, None] * inv[None, :]
    cos = jnp.cos(ang)
    sin = jnp.sin(ang)
    cos_t = jnp.concatenate([cos, cos], axis=-1)
    sin_t = jnp.concatenate([-sin, sin], axis=-1)
    scale = HEAD_B ** -0.5
    log_gamma = jnp.log1p(-(2.0 ** (-5.0 - jnp.arange(B_HEADS, dtype=F32))))
    idx = jnp.arange(CHUNK, dtype=F32)
    diff = idx[:, None] - idx[None, :]
    decay = jnp.where(diff >= 0, jnp.exp(log_gamma[:, None, None] * jnp.maximum(diff, 0.0)), 0.0)
    xi = jnp.exp(log_gamma[:, None] * (idx + 1.0))
    zeta = jnp.exp(log_gamma[:, None] * (CHUNK - 1.0 - idx))
    gamma_c = jnp.exp(log_gamma * CHUNK)
    rep = lambda t: jnp.broadcast_to(t[:, :, None], (B_HEADS, CHUNK, LANES))
    gc = jnp.broadcast_to(gamma_c[:, None, None], (B_HEADS, 1, LANES))
    return cos_t, sin_t, decay * scale, rep(xi), rep(zeta * scale), gc


def _even_mixer(x2, bsz, seq, g2, g3, w_in, w_out, v_g, v_b, w_s, b_s, ret_g):
    n, d = x2.shape
    tm = min(TOKEN_TILE, seq)
    tiles = seq // tm
    aw = A_GROUPS * CHUNK
    bw = B_HEADS * HEAD_B
    cos_t, sin_t, decay, xi, zeta, gc = _retention_tables(seq)
    bs_rep = jnp.broadcast_to(b_s[:, :, None], (A_GROUPS, CHUNK, LANES))
    row = pl.BlockSpec((tm, d), lambda b, i: (b * tiles + i, 0))
    tab = pl.BlockSpec((tm, LANES), lambda b, i: (i, 0))
    return pl.pallas_call(
        functools.partial(_even_kernel, n_chunks=tm // CHUNK),
        grid=(bsz, tiles),
        in_specs=[row, _const_spec((1, d)), _const_spec((1, d)),
                  _const_spec(w_in.shape), _const_spec(w_out.shape),
                  _const_spec((1, aw)), _const_spec((1, aw)),
                  _const_spec((A_GROUPS, CHUNK, CHUNK)), _const_spec((A_GROUPS, CHUNK, LANES)),
                  _const_spec((1, bw)), tab, tab,
                  _const_spec((B_HEADS, CHUNK, CHUNK)), _const_spec((B_HEADS, CHUNK, LANES)),
                  _const_spec((B_HEADS, CHUNK, LANES)), _const_spec((B_HEADS, 1, LANES))],
        out_specs=row,
        out_shape=jax.ShapeDtypeStruct((n, d), F32),
        scratch_shapes=[pltpu.VMEM((B_HEADS, HEAD_B, HEAD_B), F32),
                        pltpu.VMEM((tm, aw + bw), BF16)],
        compiler_params=_params("arbitrary", "arbitrary"),
        name="even_mixer",
    )(x2, g2.reshape(1, d), g3.reshape(1, d), w_in, w_out, v_g.reshape(1, aw), v_b.reshape(1, aw),
      w_s, bs_rep, ret_g.reshape(1, bw), cos_t, sin_t, decay, xi, zeta, gc)


SKIP_LOG = 88.0
MASKED_LOGIT = -1e30
SUB = 64
WIN = 2 * CHUNK


def _softplus(z):
    sign_bit = jnp.uint32(0x80000000)
    neg_abs = lax.bitcast_convert_type(lax.bitcast_convert_type(z, jnp.uint32) | sign_bit, F32)
    return jnp.maximum(z, 0.0) + jnp.log(1.0 + jnp.exp(neg_abs))


def _sb_kernel(q_ref, k_ref, v_ref, rhs_ref, o_ref, ls_scr, sp_scr, a_scr, *, group):
    seq = q_ref.shape[1]
    nsub = seq // SUB
    reach = WIN - SUB
    pair_rows = 2 * SUB
    key_blocks = WIN // CHUNK
    first = lax.broadcasted_iota(jnp.int32, (SUB, LANES), 1) < HEAD_C
    col_minus_row = (lax.broadcasted_iota(jnp.int32, (SUB, WIN), 1)
                     - lax.broadcasted_iota(jnp.int32, (SUB, WIN), 0))
    tail_col = lax.broadcasted_iota(jnp.int32, (pair_rows, CHUNK), 1)
    contract_last = (((1,), (1,)), ((), ()))

    def at(row):
        return row if isinstance(row, int) else pl.multiple_of(row, SUB)

    def load_q(u):
        qp = q_ref[0, pl.ds(at(u * SUB), SUB), :]
        zero = jnp.zeros_like(qp)
        return jnp.concatenate([jnp.where(first, qp, zero), jnp.where(first, zero, qp)], axis=0)

    def windows(subs, starts, offs):
        for g, (u, w0) in enumerate(zip(subs, starts)):
            kwin = k_ref[0, pl.ds(w0, WIN), :]
            ls_scr[g * pair_rows:(g + 1) * pair_rows, :] = lax.dot_general(
                load_q(u), kwin, contract_last, preferred_element_type=F32)
        for g, off in enumerate(offs):
            sl = slice(g * pair_rows, (g + 1) * pair_rows)
            for b in range(key_blocks):
                cols = slice(b * CHUNK, (b + 1) * CHUNK)
                z = ls_scr[sl, cols].reshape(2, SUB, CHUNK)
                if (b + 1) * CHUNK - 1 >= off:
                    z = jnp.where((col_minus_row[:, cols] < off)[None], z, MASKED_LOGIT)
                sp = _softplus(z)
                ls_scr[sl, cols] = (z - sp).reshape(pair_rows, CHUNK)
                sp_scr[b, sl, :] = sp.reshape(pair_rows, CHUNK).astype(BF16)
        r = [jnp.dot(sp_scr[b], rhs_ref[...], preferred_element_type=F32) for b in range(key_blocks)]
        acc = None
        for b in reversed(range(key_blocks)):
            later = r[b][:, :CHUNK] if acc is None else r[b][:, :CHUNK] + acc
            cols = slice(b * CHUNK, (b + 1) * CHUNK)
            a_scr[:, cols] = jnp.exp(ls_scr[:, cols] - later).astype(BF16)
            acc = r[b][:, CHUNK:] if acc is None else acc + r[b][:, CHUNK:]
        out = []
        for g, w0 in enumerate(starts):
            sl = slice(g * pair_rows, (g + 1) * pair_rows)
            vwin = v_ref[0, pl.ds(w0, WIN), :]
            out.append((acc[sl, :], jnp.dot(a_scr[sl, :], vwin, preferred_element_type=F32)))
        return out

    def tail(u, end, acc, o):
        q2 = load_q(u)

        def cond(c):
            return jnp.logical_and(c[0] > 0, c[1] < SKIP_LOG)

        def body(c):
            end, _, acc, o = c
            s0 = jnp.maximum(end - CHUNK, 0)
            kb = k_ref[0, pl.ds(at(s0), CHUNK), :]
            vb = v_ref[0, pl.ds(at(s0), CHUNK), :]
            fresh = tail_col < end - s0
            z = lax.dot_general(q2, kb, contract_last, preferred_element_type=F32)
            z = jnp.where(fresh, z, MASKED_LOGIT)
            sp = _softplus(z)
            r = jnp.dot(sp.astype(BF16), rhs_ref[...], preferred_element_type=F32)
            a = jnp.exp(z - sp - r[:, :CHUNK] - acc)
            acc = acc + r[:, CHUNK:]
            return s0, jnp.min(acc), acc, o + jnp.dot(a.astype(BF16), vb, preferred_element_type=F32)

        return lax.while_loop(cond, body, (end, jnp.min(acc), acc, o))[3]

    def store(u, o):
        o_ref[0, pl.ds(at(u * SUB), SUB), :] = jnp.where(first, o[:SUB], o[SUB:]).astype(BF16)

    def do_group(subs, starts, offs):
        states = windows(subs, starts, offs)
        for u, (_, o) in zip(subs, states):
            store(u, o)
        open_ = [(u, w0, acc, o) for u, w0, (acc, o) in zip(subs, starts, states)
                 if not (isinstance(w0, int) and w0 == 0)]
        if open_:
            lowest = functools.reduce(jnp.minimum, [acc for _, _, acc, _ in open_])

            @pl.when(jnp.min(lowest) < SKIP_LOG)
            def _():
                for u, w0, acc, o in open_:
                    store(u, tail(u, w0, acc, o))

    lead = list(range(group))
    lead_start = [max(u * SUB - reach, 0) for u in lead]
    do_group(lead, lead_start, [u * SUB - s for u, s in zip(lead, lead_start)])

    def loop_body(n, carry):
        subs = [n * group + t for t in range(group)]
        do_group(subs, [at(u * SUB - reach) for u in subs], [reach] * group)
        return carry

    lax.fori_loop(1, nsub // group, loop_body, 0)


def _stick_breaking(q, k, v, bsz, seq):
    d = q.shape[-1]
    pairs = d // LANES
    group = math.gcd(16, seq // SUB)
    assert group * SUB >= WIN
    q3, k3, v3 = (t.reshape(bsz, seq, d) for t in (q, k, v))
    idx = np.arange(CHUNK)
    later = (idx[:, None] > idx[None, :]).astype(np.float32)
    rhs = jnp.asarray(np.concatenate([later, np.ones((CHUNK, LANES), np.float32)], axis=1), BF16)
    grows = group * 2 * SUB
    spec = pl.BlockSpec((1, seq, LANES), lambda b, p: (b, 0, p))
    out = pl.pallas_call(
        functools.partial(_sb_kernel, group=group),
        grid=(bsz, pairs),
        in_specs=[spec, spec, spec, _const_spec(rhs.shape)],
        out_specs=spec,
        out_shape=jax.ShapeDtypeStruct((bsz, seq, d), BF16),
        scratch_shapes=[pltpu.VMEM((grows, WIN), F32),
                        pltpu.VMEM((WIN // CHUNK, grows, CHUNK), BF16),
                        pltpu.VMEM((grows, WIN), BF16)],
        compiler_params=_params("arbitrary", "arbitrary"),
        name="stick_breaking",
    )(q3, k3, v3, rhs)
    return out.reshape(bsz * seq, d)


def kernel(x, norm_g, ffn_w_gate, ffn_w_up, ffn_w_down, ab_w_in, ab_w_out, gmlp_v_norm_g,
           gmlp_v_norm_b, gmlp_w_s, gmlp_b_s, ret_norm_g, sb_w_qkv, sb_w_out):
    bsz, seq, d = x.shape
    depth = norm_g.shape[0]
    x2 = x.reshape(bsz * seq, d)
    weights = tuple(w.astype(BF16) for w in (ffn_w_gate, ffn_w_up, ffn_w_down))
    for layer in range(depth):
        g = norm_g[layer]
        pre = (g[0], g[1], weights, (layer, 0))
        post = (g[4], g[5], weights, (layer, 1))
        if layer % 2 == 0:
            e = layer // 2
            x2 = _ffn(x2, *pre)
            x2 = _even_mixer(x2, bsz, seq, g[2], g[3], ab_w_in[e].astype(BF16),
                             ab_w_out[e].astype(BF16), gmlp_v_norm_g[e], gmlp_v_norm_b[e],
                             gmlp_w_s[e], gmlp_b_s[e], ret_norm_g[e])
            x2 = _ffn(x2, *post)
        else:
            o = layer // 2
            x2, q, k, v = _ffn(x2, *pre, qkv=(g[2], sb_w_qkv[o].astype(BF16)))
            att = _stick_breaking(q, k, v, bsz, seq)
            x2 = _ffn(x2, *post, mix=(att, sb_w_out[o].astype(BF16), g[3]))
    return x2.reshape(bsz, seq, d)
```
